```python
import jax, jax.numpy as jnp
from jax import lax
import numpy as np

D_MODEL = 1024
BATCH = 8
SEQ = 2048
DEPTH = 2
DEC_BATCH = 128
DEC_SEQ = 1
PAST_LEN = 16384
PAGE_SIZE = 128

HG_HEADS = 8
HG_DK = 128
HG_DV = D_MODEL // HG_HEADS
HG_QK = HG_HEADS * HG_DK
HG_V = HG_HEADS * HG_DV
CHUNK = 64
CONV_DIM = D_MODEL
CONV_W = 31
D_FF = 2816
N_EXPERTS = 8
TOP_K = 2
D_FF_EXPERT = 2816
N_DENSE = (DEPTH + 1) // 2
N_MOE = DEPTH // 2
EPS = 1e-6
IN_COLS = 2 * HG_QK + 2 * HG_V + 2 * CONV_DIM + 2 * D_MODEL
IN_SPLITS = (HG_QK, 2 * HG_QK, 2 * HG_QK + HG_V, 2 * HG_QK + 2 * HG_V,
             2 * HG_QK + 2 * HG_V + CONV_DIM, 2 * HG_QK + 2 * HG_V + 2 * CONV_DIM,
             2 * HG_QK + 2 * HG_V + 2 * CONV_DIM + D_MODEL)

kernel_name = 'hgrn2_conformer_gated_hybrid_step'


def rmsnorm(x, g):
    xf = x.astype(jnp.float32)
    y = xf * lax.rsqrt(jnp.mean(xf * xf, axis=-1, keepdims=True) + EPS)
    return (y * g.astype(jnp.float32)).astype(x.dtype)


def layernorm(x, g, b):
    xf = x.astype(jnp.float32)
    mu = jnp.mean(xf, axis=-1, keepdims=True)
    xc = xf - mu
    y = xc * lax.rsqrt(jnp.mean(xc * xc, axis=-1, keepdims=True) + EPS)
    return (y * g.astype(jnp.float32) + b.astype(jnp.float32)).astype(x.dtype)


def lower_bounds(lb_logits):
    p = jax.nn.softmax(lb_logits.astype(jnp.float32), axis=0)
    return jnp.cumsum(p, axis=0) - p[0:1]


def gla_chunked(q, k, v, logf, s0):
    B, T, H, DK = q.shape
    DV = v.shape[-1]
    C = min(CHUNK, T)
    n = -(-T // C)
    pad = n * C - T
    padw = ((0, 0), (0, pad), (0, 0), (0, 0))
    q, k, v, logf = (jnp.pad(a, padw) for a in (q, k, v, logf))
    to_chunks = lambda a: a.reshape(B, n, C, H, a.shape[-1]).transpose(1, 0, 3, 2, 4)
    qc, kc, vc, lc = to_chunks(q), to_chunks(k), to_chunks(v), to_chunks(logf)
    causal = jnp.tril(jnp.ones((C, C), dtype=bool))[:, :, None]

    def step(S, inp):
        qb, kb, vb, lb = inp
        b = jnp.cumsum(lb, axis=2)
        o_inter = jnp.einsum('bhtk,bhkv->bhtv', qb * jnp.exp(b), S)
        diff = b[:, :, :, None, :] - b[:, :, None, :, :]
        decay = jnp.where(causal, jnp.exp(jnp.where(causal, diff, 0.0)), 0.0)
        att = jnp.einsum('bhtk,bhsk,bhtsk->bhts', qb, kb, decay)
        o_intra = jnp.einsum('bhts,bhsv->bhtv', att, vb)
        b_last = b[:, :, -1]
        S_new = jnp.exp(b_last)[..., None] * S + jnp.einsum(
            'bhsk,bhsv->bhkv', kb * jnp.exp(b_last[:, :, None, :] - b), vb)
        return S_new, o_inter + o_intra

    S_fin, o = lax.scan(step, s0, (qc, kc, vc, lc))
    o = o.transpose(1, 0, 3, 2, 4).reshape(B, n * C, H, DV)[:, :T]
    return o, S_fin


def causal_depthwise_conv(u, buf, w, bias):
    full = jnp.concatenate([buf.astype(u.dtype), u], axis=1)
    y = lax.conv_general_dilated(full, w.astype(u.dtype)[:, None, :], window_strides=(1,),
                                 padding='VALID', dimension_numbers=('NWC', 'WIO', 'NWC'),
                                 feature_group_count=u.shape[-1])
    return y + bias.astype(u.dtype), full[:, -(CONV_W - 1):]


def swiglu(h, wg, wu, wd):
    return (jax.nn.silu(h @ wg) * (h @ wu)) @ wd


def moe_swiglu(h, router_w, wg, wu, wd):
    B, T, D = h.shape
    t = h.reshape(B * T, D)
    logits = (t @ router_w).astype(jnp.float32)
    top_v, top_i = lax.top_k(logits, TOP_K)
    top_w = jax.nn.softmax(top_v, axis=-1)
    combine = jnp.sum(jax.nn.one_hot(top_i, N_EXPERTS, dtype=jnp.float32) * top_w[..., None], axis=1)
    out = jnp.zeros_like(t)
    for e in range(N_EXPERTS):
        out = out + combine[:, e:e + 1].astype(t.dtype) * swiglu(t, wg[e], wu[e], wd[e])
    return out.reshape(B, T, D)


def trunk(x, c, s_hgrn, s_conv, P):
    B, T, _ = x.shape
    f32 = jnp.float32
    lb_all = lower_bounds(P['lb_logits'])
    c_act = jax.nn.silu(c)
    new_h, new_c = [], []
    for l in range(DEPTH):
        mod = (c_act @ P['ada_w'][l] + P['ada_b'][l])[:, None, :]
        sh1, sc1, g1, sh2, sc2, g2 = jnp.split(mod, 6, axis=-1)
        h = rmsnorm(x, P['norm1_g'][l]) * (1 + sc1) + sh1
        z = h @ P['w_in'][l]
        zq, zf, zi, zg, za, zb, gate_a, gate_b = jnp.split(z, IN_SPLITS, axis=-1)
        lb = lb_all[l]
        ff = zf.astype(f32)
        k = (1 - lb) * jax.nn.sigmoid(-ff)
        logf = jnp.log(lb + (1 - lb) * jax.nn.sigmoid(ff))
        q = jax.nn.silu(zq.astype(f32))
        heads = lambda a, d: a.reshape(B, T, HG_HEADS, d)
        o, s_out = gla_chunked(heads(q, HG_DK), heads(k, HG_DK), heads(zi.astype(f32), HG_DV),
                               heads(logf, HG_DK), s_hgrn[l].astype(f32))
        o = o * lax.rsqrt(jnp.mean(o * o, axis=-1, keepdims=True) + EPS)
        o = o.reshape(B, T, HG_V) * P['hg_norm_g'][l].astype(f32)
        o = o.astype(x.dtype) * jax.nn.silu(zg)
        y_a = o @ P['w_hg_out'][l]
        u = za * jax.nn.sigmoid(zb)
        v, buf = causal_depthwise_conv(u, s_conv[l], P['conv_w'][l], P['conv_b'][l])
        v = jax.nn.silu(layernorm(v, P['conv_ln_g'][l], P['conv_ln_b'][l]))
        y_b = v @ P['w_conv_out'][l]
        mixed = jax.nn.sigmoid(gate_a) * y_a + jax.nn.sigmoid(gate_b) * y_b
        x = x + g1 * (mixed @ P['w_o'][l])
        h2 = rmsnorm(x, P['norm2_g'][l]) * (1 + sc2) + sh2
        j = l // 2
        if l % 2 == 0:
            f_out = swiglu(h2, P['ffn_w_gate'][j], P['ffn_w_up'][j], P['ffn_w_down'][j])
        else:
            f_out = moe_swiglu(h2, P['router_w'][j], P['moe_w_gate'][j], P['moe_w_up'][j],
                               P['moe_w_down'][j])
        x = x + g2 * f_out
        new_h.append(s_out)
        new_c.append(buf)
    return rmsnorm(x, P['final_norm_g']), jnp.stack(new_h), jnp.stack(new_c)


def setup_inputs(seed: int = 0) -> dict:
    key = jax.random.key(seed)
    ks = iter(jax.random.split(key, 40))
    nrm = lambda shape, s: jax.random.normal(next(ks), shape, jnp.float32) * s
    D = D_MODEL
    return {
        'x_prompt': nrm((BATCH, SEQ, D), 1.0),
        'x_sample': nrm((DEC_BATCH, DEC_SEQ, D), 1.0),
        'state_hgrn': nrm((DEPTH, DEC_BATCH, HG_HEADS, HG_DK, HG_DV), 0.2),
        'state_conv': nrm((DEPTH, DEC_BATCH, CONV_W - 1, CONV_DIM), 0.5),
        'c_prompt': nrm((BATCH, D), 1.0),
        'c_sample': nrm((DEC_BATCH, D), 1.0),
        'ada_w': nrm((DEPTH, D, 6 * D), 0.5 * D ** -0.5),
        'ada_b': nrm((DEPTH, 6 * D), 0.02),
        'norm1_g': 1.0 + nrm((DEPTH, D), 0.02),
        'norm2_g': 1.0 + nrm((DEPTH, D), 0.02),
        'w_in': nrm((DEPTH, D, IN_COLS), D ** -0.5),
        'lb_logits': nrm((DEPTH, HG_QK), 0.5),
        'hg_norm_g': 1.0 + nrm((DEPTH, HG_V), 0.02),
        'w_hg_out': nrm((DEPTH, HG_V, D), HG_V ** -0.5),
        'conv_w': nrm((DEPTH, CONV_W, CONV_DIM), CONV_W ** -0.5),
        'conv_b': nrm((DEPTH, CONV_DIM), 0.02),
        'conv_ln_g': 1.0 + nrm((DEPTH, CONV_DIM), 0.02),
        'conv_ln_b': nrm((DEPTH, CONV_DIM), 0.02),
        'w_conv_out': nrm((DEPTH, CONV_DIM, D), CONV_DIM ** -0.5),
        'w_o': nrm((DEPTH, D, D), D ** -0.5),
        'ffn_w_gate': nrm((N_DENSE, D, D_FF), D ** -0.5),
        'ffn_w_up': nrm((N_DENSE, D, D_FF), D ** -0.5),
        'ffn_w_down': nrm((N_DENSE, D_FF, D), D_FF ** -0.5),
        'router_w': nrm((N_MOE, D, N_EXPERTS), D ** -0.5),
        'moe_w_gate': nrm((N_MOE, N_EXPERTS, D, D_FF_EXPERT), D ** -0.5),
        'moe_w_up': nrm((N_MOE, N_EXPERTS, D, D_FF_EXPERT), D ** -0.5),
        'moe_w_down': nrm((N_MOE, N_EXPERTS, D_FF_EXPERT, D), D_FF_EXPERT ** -0.5),
        'final_norm_g': 1.0 + nrm((D,), 0.02),
    }


def reference(x_prompt, x_sample, state_hgrn, state_conv, c_prompt, c_sample,
              ada_w, ada_b, norm1_g, norm2_g, w_in, lb_logits, hg_norm_g, w_hg_out,
              conv_w, conv_b, conv_ln_g, conv_ln_b, w_conv_out, w_o,
              ffn_w_gate, ffn_w_up, ffn_w_down, router_w, moe_w_gate, moe_w_up, moe_w_down,
              final_norm_g):
    P = {'ada_w': ada_w, 'ada_b': ada_b, 'norm1_g': norm1_g, 'norm2_g': norm2_g,
         'w_in': w_in, 'lb_logits': lb_logits, 'hg_norm_g': hg_norm_g, 'w_hg_out': w_hg_out,
         'conv_w': conv_w, 'conv_b': conv_b, 'conv_ln_g': conv_ln_g, 'conv_ln_b': conv_ln_b,
         'w_conv_out': w_conv_out, 'w_o': w_o, 'ffn_w_gate': ffn_w_gate, 'ffn_w_up': ffn_w_up,
         'ffn_w_down': ffn_w_down, 'router_w': router_w, 'moe_w_gate': moe_w_gate,
         'moe_w_up': moe_w_up, 'moe_w_down': moe_w_down, 'final_norm_g': final_norm_g}
    bp = x_prompt.shape[0]
    h0 = jnp.zeros((DEPTH, bp, HG_HEADS, HG_DK, HG_DV), jnp.float32)
    c0 = jnp.zeros((DEPTH, bp, CONV_W - 1, CONV_DIM), x_prompt.dtype)
    y_prompt, hgrn_prompt, conv_prompt = trunk(x_prompt, c_prompt, h0, c0, P)
    y_sample, hgrn_sample, conv_sample = trunk(x_sample, c_sample, state_hgrn, state_conv, P)
    return (y_prompt, y_sample, hgrn_prompt, conv_prompt, hgrn_sample, conv_sample)
```

```python
import functools

import numpy as np
import jax
import jax.numpy as jnp
from jax import lax
from jax.experimental import pallas as pl
from jax.experimental.pallas import tpu as pltpu

F32 = jnp.float32
BF16 = jnp.bfloat16

D_MODEL = 1024
DEPTH = 2
HEADS = 8
HEAD_DIM = 128
CHUNK = 64
CONV_W = 31
N_EXPERTS = 8
D_FF = 2816
EPS = 1e-6
LANES = 128
N_GROUPS_OUT = 7
VMEM_LIMIT = 48 * 1024 * 1024

LEVEL_WIDTHS = tuple(CHUNK >> (i + 1) for i in range(CHUNK.bit_length() - 1))


def _sigmoid(x):
    return 1.0 / (1.0 + jnp.exp(-x))


def _silu(x):
    return x * _sigmoid(x)


def _cparams(sem):
    return pltpu.CompilerParams(dimension_semantics=sem, vmem_limit_bytes=VMEM_LIMIT)


def _ada_kernel(c_ref, w_ref, b_ref, o_ref):
    a = _silu(c_ref[...]).astype(BF16)
    o_ref[...] = jnp.dot(a, w_ref[...].astype(BF16), preferred_element_type=F32) + b_ref[...]


def _ada(c_all, ada_w, ada_b):
    n = c_all.shape[0]
    tn = 1024
    return pl.pallas_call(
        _ada_kernel,
        grid=(DEPTH, 6 * D_MODEL // tn),
        in_specs=[
            pl.BlockSpec((n, D_MODEL), lambda l, j: (0, 0)),
            pl.BlockSpec((None, D_MODEL, tn), lambda l, j: (l, 0, j)),
            pl.BlockSpec((None, 1, tn), lambda l, j: (l, 0, j)),
        ],
        out_specs=pl.BlockSpec((None, n, tn), lambda l, j: (l, 0, j)),
        out_shape=jax.ShapeDtypeStruct((DEPTH, n, 6 * D_MODEL), F32),
        compiler_params=_cparams(("parallel", "parallel")),
        name="ada",
    )(c_all, ada_w, ada_b.reshape(DEPTH, 1, 6 * D_MODEL))


def _inproj_kernel(x_ref, sh_ref, sc_ref, g_ref, lbl_ref, w_ref, z_ref, lf_ref, h_scr, *, layer):
    j = pl.program_id(1)

    @pl.when(j == 0)
    def _():
        x = x_ref[...]
        y = x * lax.rsqrt(jnp.mean(x * x, axis=-1, keepdims=True) + EPS) * g_ref[...]
        h_scr[...] = (y * (1.0 + sc_ref[...]) + sh_ref[...]).astype(BF16)

    z = jnp.dot(h_scr[...], w_ref[...], preferred_element_type=F32)

    @pl.when(j == 0)
    def _():
        z_ref[...] = _silu(z)

    @pl.when(j == 1)
    def _():
        lbl = lbl_ref[...]
        e = jnp.exp(lbl - jnp.max(lbl, axis=0, keepdims=True))
        p = e / jnp.sum(e, axis=0, keepdims=True)
        cum = p[0:1]
        for i in range(1, layer + 1):
            cum = cum + p[i:i + 1]
        lb = cum - p[0:1]
        z_ref[...] = (1.0 - lb) * _sigmoid(-z)
        lf_ref[...] = jnp.log(lb + (1.0 - lb) * _sigmoid(z))

    @pl.when(j == 2)
    def _():
        z_ref[...] = z

    @pl.when(j == 3)
    def _():
        z_ref[...] = _silu(z)

    @pl.when(j == 4)
    def _():
        z_ref[...] = z

    @pl.when(j == 5)
    def _():
        z_ref[...] = z_ref[...] * _sigmoid(z)

    @pl.when(j >= 6)
    def _():
        z_ref[...] = _sigmoid(z)


def _inproj(x, mod, norm_g, lb_logits, w_in_b, layer, tm, rows_per_mod):
    m = x.shape[0]
    mod_rows = mod.shape[1]

    def mod_spec(col):
        return pl.BlockSpec((None, mod_rows, D_MODEL),
                            lambda i, j: ((i * tm) // rows_per_mod, 0, col))

    return pl.pallas_call(
        functools.partial(_inproj_kernel, layer=layer),
        grid=(m // tm, 8),
        in_specs=[
            pl.BlockSpec((tm, D_MODEL), lambda i, j: (i, 0)),
            mod_spec(0), mod_spec(1),
            pl.BlockSpec((1, D_MODEL), lambda i, j: (0, 0)),
            pl.BlockSpec((DEPTH, D_MODEL), lambda i, j: (0, 0)),
            pl.BlockSpec((None, D_MODEL, D_MODEL), lambda i, j: (layer, 0, j)),
        ],
        out_specs=[
            pl.BlockSpec((tm, D_MODEL), lambda i, j: (i, jnp.where(j >= 5, j - 1, j))),
            pl.BlockSpec((tm, D_MODEL), lambda i, j: (i, 0)),
        ],
        out_shape=[
            jax.ShapeDtypeStruct((m, N_GROUPS_OUT * D_MODEL), F32),
            jax.ShapeDtypeStruct((m, D_MODEL), F32),
        ],
        scratch_shapes=[pltpu.VMEM((tm, D_MODEL), BF16)],
        compiler_params=_cparams(("parallel", "arbitrary")),
        name="inproj",
    )(x, mod, mod, norm_g.reshape(1, D_MODEL), lb_logits, w_in_b)


def _hgrn_constants():
    t = np.arange(CHUNK)
    blocks = [(t[None, :] <= t[:, None])]
    for w in LEVEL_WIDTHS:
        ref_row = (t & ~(2 * w - 1)) + w - 1
        blocks.append(t[None, :] <= ref_row[:, None])
    cum = np.concatenate(blocks, axis=0).astype(np.float32)
    acat = np.concatenate([cum, cum, cum, np.zeros_like(cum)], axis=1)
    x = t[:, None] ^ t[None, :]
    lvl = np.full((CHUNK, CHUNK), -1, np.int32)
    for li, w in enumerate(LEVEL_WIDTHS):
        lvl[((x // w) == 1) & ((t[:, None] & w) != 0)] = li
    lvl[t[:, None] == t[None, :]] = len(LEVEL_WIDTHS)
    return jnp.asarray(acat, BF16), jnp.asarray(lvl)


def _hgrn_prompt_kernel(q_ref, k_ref, v_ref, og_ref, lf_ref, gn_ref, acat_ref, lvl_ref,
                        o_ref, s_ref, st_scr, *, n_chunks):
    tb = pl.program_id(1)

    @pl.when(tb == 0)
    def _():
        st_scr[...] = jnp.zeros_like(st_scr)

    lvl = lvl_ref[...]
    row = lax.broadcasted_iota(jnp.int32, (CHUNK, HEAD_DIM), 0)
    trans_b = (((1,), (1,)), ((), ()))
    trans_a = (((0,), (0,)), ((), ()))

    def chunk_body(c, carry):
        r0 = pl.multiple_of(c * CHUNK, CHUNK)
        for h in range(HEADS):
            sl = slice(h * HEAD_DIM, (h + 1) * HEAD_DIM)
            lf = lf_ref[pl.ds(r0, CHUNK), sl]
            q = q_ref[pl.ds(r0, CHUNK), sl]
            k = k_ref[pl.ds(r0, CHUNK), sl]
            v = v_ref[pl.ds(r0, CHUNK), sl].astype(BF16)
            hi = lf.astype(BF16)
            r1 = lf - hi.astype(F32)
            mid = r1.astype(BF16)
            lo = (r1 - mid.astype(F32)).astype(BF16)
            lf4 = jnp.concatenate([hi, mid, lo, jnp.zeros_like(hi)], axis=0)
            cums = jnp.dot(acat_ref[...], lf4, preferred_element_type=F32)
            b = cums[0:CHUNK]
            att = jnp.where(
                lvl == len(LEVEL_WIDTHS),
                lax.dot_general(q.astype(BF16), k.astype(BF16), trans_b,
                                preferred_element_type=F32), 0.0)
            for li, w in enumerate(LEVEL_WIDTHS):
                ref = cums[(li + 1) * CHUNK:(li + 2) * CHUNK]
                xs = (jnp.where((row & w) != 0, q, k) * jnp.exp(-jnp.abs(b - ref))).astype(BF16)
                g = lax.dot_general(xs, xs, trans_b, preferred_element_type=F32)
                att = jnp.where(lvl == li, g, att)
            st = st_scr[h]
            inter = lax.dot_general((q * jnp.exp(b)).astype(BF16), st.astype(BF16), trans_b,
                                    preferred_element_type=F32)
            o = inter + jnp.dot(att.astype(BF16), v, preferred_element_type=F32)
            b_last = b[CHUNK - 1:CHUNK]
            kd = (k * jnp.exp(b_last - b)).astype(BF16)
            st_scr[h] = jnp.exp(b_last) * st + lax.dot_general(
                v, kd, trans_a, preferred_element_type=F32)
            o = o * lax.rsqrt(jnp.mean(o * o, axis=-1, keepdims=True) + EPS)
            o = o * gn_ref[:, sl] * og_ref[pl.ds(r0, CHUNK), sl]
            o_ref[pl.ds(r0, CHUNK), sl] = o.astype(BF16)
        return carry

    lax.fori_loop(0, n_chunks, chunk_body, 0)

    @pl.when(tb == pl.num_programs(1) - 1)
    def _():
        for h in range(HEADS):
            s_ref[h] = st_scr[h].T


def _hgrn_prompt(z3, lf3, gn, acat, lvl, tb):
    bsz, t, _ = z3.shape

    def zspec(col):
        return pl.BlockSpec((None, tb, D_MODEL), lambda b, i: (b, i, col))

    return pl.pallas_call(
        functools.partial(_hgrn_prompt_kernel, n_chunks=tb // CHUNK),
        grid=(bsz, t // tb),
        in_specs=[
            zspec(0), zspec(1), zspec(2), zspec(3),
            pl.BlockSpec((None, tb, D_MODEL), lambda b, i: (b, i, 0)),
            pl.BlockSpec((1, D_MODEL), lambda b, i: (0, 0)),
            pl.BlockSpec(acat.shape, lambda b, i: (0, 0)),
            pl.BlockSpec(lvl.shape, lambda b, i: (0, 0)),
        ],
        out_specs=[
            pl.BlockSpec((None, tb, D_MODEL), lambda b, i: (b, i, 0)),
            pl.BlockSpec((None, HEADS, HEAD_DIM, HEAD_DIM), lambda b, i: (b, 0, 0, 0)),
        ],
        out_shape=[
            jax.ShapeDtypeStruct((bsz, t, D_MODEL), BF16),
            jax.ShapeDtypeStruct((bsz, HEADS, HEAD_DIM, HEAD_DIM), F32),
        ],
        scratch_shapes=[pltpu.VMEM((HEADS, HEAD_DIM, HEAD_DIM), F32)],
        compiler_params=_cparams(("parallel", "arbitrary")),
        name="hgrn_prompt",
    )(z3, z3, z3, z3, lf3, gn.reshape(1, D_MODEL), acat, lvl)


def _hgrn_sample_kernel(q_ref, k_ref, v_ref, og_ref, lf_ref, gn_ref, s_ref, o_ref, so_ref, o_scr,
                        *, bb):
    n = bb * HEADS
    qt = q_ref[...].reshape(n, HEAD_DIM).T
    kt = k_ref[...].reshape(n, HEAD_DIM).T
    ft = jnp.exp(lf_ref[...].reshape(n, HEAD_DIM)).T
    for b in range(bb):
        for h in range(HEADS):
            c = b * HEADS + h
            s_new = ft[:, c:c + 1] * s_ref[b, h] + kt[:, c:c + 1] * v_ref[b, h:h + 1, :]
            so_ref[b, h] = s_new
            o_scr[c:c + 1, :] = jnp.sum(qt[:, c:c + 1] * s_new, axis=0, keepdims=True)
    o = o_scr[...]
    o = o * lax.rsqrt(jnp.mean(o * o, axis=-1, keepdims=True) + EPS)
    o = o * gn_ref[...] * og_ref[...].reshape(n, HEAD_DIM)
    o_ref[...] = o.astype(BF16)


def _hgrn_sample(z, lf, gn, state, bb):
    n = z.shape[0]
    z3 = z.reshape(n, N_GROUPS_OUT * HEADS, HEAD_DIM)
    lf3 = lf.reshape(n, HEADS, HEAD_DIM)
    gn_rows = jnp.tile(gn.reshape(HEADS, HEAD_DIM), (bb, 1))

    def zspec(col):
        return pl.BlockSpec((bb, HEADS, HEAD_DIM), lambda i: (i, col, 0))

    state_spec = pl.BlockSpec((bb, HEADS, HEAD_DIM, HEAD_DIM), lambda i: (i, 0, 0, 0))
    o, s_new = pl.pallas_call(
        functools.partial(_hgrn_sample_kernel, bb=bb),
        grid=(n // bb,),
        in_specs=[
            zspec(0), zspec(1), zspec(2), zspec(3),
            pl.BlockSpec((bb, HEADS, HEAD_DIM), lambda i: (i, 0, 0)),
            pl.BlockSpec((bb * HEADS, HEAD_DIM), lambda i: (0, 0)),
            state_spec,
        ],
        out_specs=[
            pl.BlockSpec((bb * HEADS, HEAD_DIM), lambda i: (i, 0)),
            state_spec,
        ],
        out_shape=[
            jax.ShapeDtypeStruct((n * HEADS, HEAD_DIM), BF16),
            jax.ShapeDtypeStruct(state.shape, F32),
        ],
        scratch_shapes=[pltpu.VMEM((bb * HEADS, HEAD_DIM), F32)],
        compiler_params=_cparams(("parallel",)),
        name="hgrn_sample",
    )(z3, z3, z3, z3, lf3, gn_rows, state)
    return o.reshape(n, D_MODEL), s_new


CONV_HIST = 32
CONV_ROWS = 16


def _conv_prompt_kernel(u_ref, wb_ref, bias_ref, y_ref, buf, shifted, *, tc):
    @pl.when(pl.program_id(1) == 0)
    def _():
        buf[0:CONV_HIST, :] = jnp.zeros((CONV_HIST, D_MODEL), F32)

    buf[CONV_HIST:CONV_HIST + tc, :] = u_ref[...]
    lead = CONV_HIST - (CONV_W - 1)
    n_shift = shifted.shape[1]
    for s in range(1, 8):
        shifted[s - 1] = buf[s:s + n_shift, :]

    def rows_body(r, carry):
        base = pl.multiple_of(r * CONV_ROWS, CONV_ROWS)
        accs = [jnp.broadcast_to(bias_ref[...], (8, D_MODEL)) for _ in range(CONV_ROWS // 8)]
        for j in range(CONV_W):
            wj = wb_ref[j * 8:(j + 1) * 8, :]
            p, s = divmod(lead + j, 8)
            for a in range(CONV_ROWS // 8):
                rows = pl.ds(base + 8 * (a + p), 8)
                tap = buf[rows, :] if s == 0 else shifted[s - 1, rows, :]
                accs[a] = accs[a] + wj * tap
        for a in range(CONV_ROWS // 8):
            y_ref[pl.ds(base + a * 8, 8), :] = accs[a]
        return carry

    lax.fori_loop(0, tc // CONV_ROWS, rows_body, 0)
    buf[0:CONV_HIST, :] = buf[tc:tc + CONV_HIST, :]


def _conv_prompt(z3, conv_w, conv_b, tc):
    bsz, t, _ = z3.shape
    wb = jnp.repeat(conv_w, 8, axis=0)
    return pl.pallas_call(
        functools.partial(_conv_prompt_kernel, tc=tc),
        grid=(bsz, t // tc),
        in_specs=[
            pl.BlockSpec((None, tc, D_MODEL), lambda b, i: (b, i, 4)),
            pl.BlockSpec((CONV_W * 8, D_MODEL), lambda b, i: (0, 0)),
            pl.BlockSpec((1, D_MODEL), lambda b, i: (0, 0)),
        ],
        out_specs=pl.BlockSpec((None, tc, D_MODEL), lambda b, i: (b, i, 0)),
        out_shape=jax.ShapeDtypeStruct((bsz, t, D_MODEL), F32),
        scratch_shapes=[pltpu.VMEM((CONV_HIST + tc, D_MODEL), F32),
                        pltpu.VMEM((7, CONV_HIST + tc - 8, D_MODEL), F32)],
        compiler_params=_cparams(("parallel", "arbitrary")),
        name="conv_prompt",
    )(z3, wb, conv_b.reshape(1, D_MODEL))


def _conv_sample_kernel(u_ref, s_ref, w_ref, bias_ref, y_ref, so_ref, *, bb):
    hist = CONV_W - 1
    for b in range(bb):
        u = u_ref[b:b + 1, :]
        y = jnp.sum(s_ref[b] * w_ref[0:hist, :], axis=0, keepdims=True)
        y_ref[b:b + 1, :] = y + w_ref[hist:CONV_W, :] * u + bias_ref[...]
        so_ref[b, 0:hist - 1, :] = s_ref[b, 1:hist, :]
        so_ref[b, hist - 1:hist, :] = u


def _conv_sample(z, state, conv_w, conv_b, bb):
    n = z.shape[0]
    hist = CONV_W - 1
    state_spec = pl.BlockSpec((bb, hist, D_MODEL), lambda i: (i, 0, 0))
    return pl.pallas_call(
        functools.partial(_conv_sample_kernel, bb=bb),
        grid=(n // bb,),
        in_specs=[
            pl.BlockSpec((bb, D_MODEL), lambda i: (i, 4)),
            state_spec,
            pl.BlockSpec((CONV_W, D_MODEL), lambda i: (0, 0)),
            pl.BlockSpec((1, D_MODEL), lambda i: (0, 0)),
        ],
        out_specs=[pl.BlockSpec((bb, D_MODEL), lambda i: (i, 0)), state_spec],
        out_shape=[jax.ShapeDtypeStruct((n, D_MODEL), F32),
                   jax.ShapeDtypeStruct(state.shape, F32)],
        compiler_params=_cparams(("parallel",)),
        name="conv_sample",
    )(z, state, conv_w, conv_b.reshape(1, D_MODEL))


def _merge_kernel(yc_ref, oa_ref, ga_ref, gb_ref, x_ref, g1_ref, sh2_ref, sc2_ref,
                  lng_ref, lnb_ref, n2g_ref, wco_ref, whg_ref, wo_ref, x1_ref, h2_ref):
    yc = yc_ref[...]
    xc = yc - jnp.mean(yc, axis=-1, keepdims=True)
    vln = xc * lax.rsqrt(jnp.mean(xc * xc, axis=-1, keepdims=True) + EPS)
    va = _silu(vln * lng_ref[...] + lnb_ref[...]).astype(BF16)
    y_b = jnp.dot(va, wco_ref[...], preferred_element_type=F32)
    y_a = jnp.dot(oa_ref[...], whg_ref[...], preferred_element_type=F32)
    mixed = (ga_ref[...] * y_a + gb_ref[...] * y_b).astype(BF16)
    x1 = x_ref[...] + g1_ref[...] * jnp.dot(mixed, wo_ref[...], preferred_element_type=F32)
    x1_ref[...] = x1
    y = x1 * lax.rsqrt(jnp.mean(x1 * x1, axis=-1, keepdims=True) + EPS) * n2g_ref[...]
    h2_ref[...] = (y * (1.0 + sc2_ref[...]) + sh2_ref[...]).astype(BF16)


def _merge(yc, oa, z, x, mod, ln_g, ln_b, n2_g, wco, whg, wo, layer, tm, rows_per_mod):
    m = x.shape[0]
    mod_rows = mod.shape[1]

    def mod_spec(col):
        return pl.BlockSpec((None, mod_rows, D_MODEL),
                            lambda i: ((i * tm) // rows_per_mod, 0, col))

    def row_spec(col=0):
        return pl.BlockSpec((tm, D_MODEL), lambda i: (i, col))

    vec_spec = pl.BlockSpec((1, D_MODEL), lambda i: (0, 0))
    w_spec = pl.BlockSpec((None, D_MODEL, D_MODEL), lambda i: (layer, 0, 0))
    return pl.pallas_call(
        _merge_kernel,
        grid=(m // tm,),
        in_specs=[row_spec(), row_spec(), row_spec(5), row_spec(6), row_spec(),
                  mod_spec(2), mod_spec(3), mod_spec(4),
                  vec_spec, vec_spec, vec_spec, w_spec, w_spec, w_spec],
        out_specs=[row_spec(), row_spec()],
        out_shape=[jax.ShapeDtypeStruct((m, D_MODEL), F32),
                   jax.ShapeDtypeStruct((m, D_MODEL), BF16)],
        compiler_params=_cparams(("parallel",)),
        name="merge",
    )(yc, oa, z, z, x, mod, mod, mod,
      ln_g.reshape(1, D_MODEL), ln_b.reshape(1, D_MODEL), n2_g.reshape(1, D_MODEL),
      wco, whg, wo)


def _ffn_kernel(h_ref, x1_ref, g2_ref, wg_ref, wu_ref, wd_ref, o_ref, acc):
    j = pl.program_id(1)

    @pl.when(j == 0)
    def _():
        acc[...] = jnp.zeros_like(acc)

    h = h_ref[...]
    a = _silu(jnp.dot(h, wg_ref[...], preferred_element_type=F32))
    a = a * jnp.dot(h, wu_ref[...], preferred_element_type=F32)
    acc[...] += jnp.dot(a.astype(BF16), wd_ref[...], preferred_element_type=F32)

    @pl.when(j == pl.num_programs(1) - 1)
    def _():
        o_ref[...] = x1_ref[...] + g2_ref[...] * acc[...]


def _ffn(h2, x1, mod, wg, wu, wd, tm, tf, rows_per_mod):
    m = x1.shape[0]
    mod_rows = mod.shape[1]
    return pl.pallas_call(
        _ffn_kernel,
        grid=(m // tm, D_FF // tf),
        in_specs=[
            pl.BlockSpec((tm, D_MODEL), lambda i, j: (i, 0)),
            pl.BlockSpec((tm, D_MODEL), lambda i, j: (i, 0)),
            pl.BlockSpec((None, mod_rows, D_MODEL),
                         lambda i, j: ((i * tm) // rows_per_mod, 0, 5)),
            pl.BlockSpec((None, D_MODEL, tf), lambda i, j: (0, 0, j)),
            pl.BlockSpec((None, D_MODEL, tf), lambda i, j: (0, 0, j)),
            pl.BlockSpec((None, tf, D_MODEL), lambda i, j: (0, j, 0)),
        ],
        out_specs=pl.BlockSpec((tm, D_MODEL), lambda i, j: (i, 0)),
        out_shape=jax.ShapeDtypeStruct((m, D_MODEL), F32),
        scratch_shapes=[pltpu.VMEM((tm, D_MODEL), F32)],
        compiler_params=_cparams(("parallel", "arbitrary")),
        name="ffn",
    )(h2, x1, mod, wg, wu, wd)


def _moe_kernel(h_ref, x1_ref, g2_ref, rw_ref, wg_ref, wu_ref, wd_ref, fg_ref, o_ref,
                acc, comb):
    e = pl.program_id(1)
    j = pl.program_id(2)
    lane = lax.broadcasted_iota(jnp.int32, comb.shape, 1)

    @pl.when((e == 0) & (j == 0))
    def _():
        acc[...] = jnp.zeros_like(acc)
        logits = jnp.dot(h_ref[...], rw_ref[...], preferred_element_type=F32)
        neg = jnp.float32(-jnp.inf)
        logits = jnp.where(lane < N_EXPERTS, logits, neg)
        v1 = jnp.max(logits, axis=-1, keepdims=True)
        i1 = jnp.min(jnp.where(logits == v1, lane, LANES), axis=-1, keepdims=True)
        rest = jnp.where(lane == i1, neg, logits)
        v2 = jnp.max(rest, axis=-1, keepdims=True)
        i2 = jnp.min(jnp.where(rest == v2, lane, LANES), axis=-1, keepdims=True)
        e2 = jnp.exp(v2 - v1)
        w1 = 1.0 / (1.0 + e2)
        comb[...] = jnp.where(lane == i1, w1, 0.0) + jnp.where(lane == i2, e2 * w1, 0.0)

    ce = jnp.sum(jnp.where(lane == e, comb[...], 0.0), axis=-1, keepdims=True)
    h = h_ref[...]
    a = _silu(jnp.dot(h, wg_ref[...], preferred_element_type=F32))
    a = a * jnp.dot(h, wu_ref[...], preferred_element_type=F32) * ce
    acc[...] += jnp.dot(a.astype(BF16), wd_ref[...], preferred_element_type=F32)

    @pl.when((e == pl.num_programs(1) - 1) & (j == pl.num_programs(2) - 1))
    def _():
        x2 = x1_ref[...] + g2_ref[...] * acc[...]
        o_ref[...] = x2 * lax.rsqrt(jnp.mean(x2 * x2, axis=-1, keepdims=True) + EPS) * fg_ref[...]


def _moe(h2, x1, mod, router_pad, wg, wu, wd, final_g, tm, tf, rows_per_mod):
    m = x1.shape[0]
    mod_rows = mod.shape[1]
    return pl.pallas_call(
        _moe_kernel,
        grid=(m // tm, N_EXPERTS, D_FF // tf),
        in_specs=[
            pl.BlockSpec((tm, D_MODEL), lambda i, e, j: (i, 0)),
            pl.BlockSpec((tm, D_MODEL), lambda i, e, j: (i, 0)),
            pl.BlockSpec((None, mod_rows, D_MODEL),
                         lambda i, e, j: ((i * tm) // rows_per_mod, 0, 5)),
            pl.BlockSpec((D_MODEL, LANES), lambda i, e, j: (0, 0)),
            pl.BlockSpec((None, None, D_MODEL, tf), lambda i, e, j: (0, e, 0, j)),
            pl.BlockSpec((None, None, D_MODEL, tf), lambda i, e, j: (0, e, 0, j)),
            pl.BlockSpec((None, None, tf, D_MODEL), lambda i, e, j: (0, e, j, 0)),
            pl.BlockSpec((1, D_MODEL), lambda i, e, j: (0, 0)),
        ],
        out_specs=pl.BlockSpec((tm, D_MODEL), lambda i, e, j: (i, 0)),
        out_shape=jax.ShapeDtypeStruct((m, D_MODEL), F32),
        scratch_shapes=[pltpu.VMEM((tm, D_MODEL), F32), pltpu.VMEM((tm, LANES), F32)],
        compiler_params=_cparams(("parallel", "arbitrary", "arbitrary")),
        name="moe",
    )(h2, x1, mod, router_pad, wg, wu, wd, final_g.reshape(1, D_MODEL))


def kernel(x_prompt, x_sample, state_hgrn, state_conv, c_prompt, c_sample, ada_w, ada_b, norm1_g, norm2_g, w_in, lb_logits, hg_norm_g, w_hg_out, conv_w, conv_b, conv_ln_g, conv_ln_b, w_conv_out, w_o, ffn_w_gate, ffn_w_up, ffn_w_down, router_w, moe_w_gate, moe_w_up, moe_w_down, final_norm_g):
    bsz, seq, _ = x_prompt.shape
    n_s = x_sample.shape[0]
    hist = CONV_W - 1

    w_in_b = w_in.astype(BF16)
    whg_b = w_hg_out.astype(BF16)
    wco_b = w_conv_out.astype(BF16)
    wo_b = w_o.astype(BF16)
    ffn_g, ffn_u, ffn_d = (w.astype(BF16) for w in (ffn_w_gate, ffn_w_up, ffn_w_down))
    moe_g, moe_u, moe_d = (w.astype(BF16) for w in (moe_w_gate, moe_w_up, moe_w_down))
    router_pad = jnp.pad(router_w[0], ((0, 0), (0, LANES - N_EXPERTS))).astype(BF16)
    acat, lvl = _hgrn_constants()

    mod_all = _ada(jnp.concatenate([c_prompt, c_sample], axis=0), ada_w, ada_b)

    xp = x_prompt.reshape(bsz * seq, D_MODEL)
    xs = x_sample.reshape(n_s, D_MODEL)
    tm_p = 512
    hgrn_p, conv_p, hgrn_s, conv_s = [], [], [], []
    for l in range(DEPTH):
        mod_p = mod_all[l, :bsz].reshape(bsz, 1, 6 * D_MODEL)
        mod_s = mod_all[l, bsz:].reshape(1, n_s, 6 * D_MODEL)

        zp, lfp = _inproj(xp, mod_p, norm1_g[l], lb_logits, w_in_b, l, tm_p, seq)
        zs, lfs = _inproj(xs, mod_s, norm1_g[l], lb_logits, w_in_b, l, n_s, n_s)

        zp3 = zp.reshape(bsz, seq, N_GROUPS_OUT * D_MODEL)
        oa_p, s_p = _hgrn_prompt(zp3, lfp.reshape(bsz, seq, D_MODEL), hg_norm_g[l], acat, lvl, 256)
        oa_s, s_s = _hgrn_sample(zs, lfs, hg_norm_g[l], state_hgrn[l], 8)

        yc_p = _conv_prompt(zp3, conv_w[l], conv_b[l], 256)
        yc_s, c_s = _conv_sample(zs, state_conv[l], conv_w[l], conv_b[l], 8)
        c_p = zp3[:, seq - hist:, 4 * D_MODEL:5 * D_MODEL]

        merge_w = (conv_ln_g[l], conv_ln_b[l], norm2_g[l], wco_b, whg_b, wo_b, l)
        x1p, h2p = _merge(yc_p.reshape(bsz * seq, D_MODEL), oa_p.reshape(bsz * seq, D_MODEL),
                          zp, xp, mod_p, *merge_w, 256, seq)
        x1s, h2s = _merge(yc_s, oa_s, zs, xs, mod_s, *merge_w, n_s, n_s)

        if l % 2 == 0:
            xp = _ffn(h2p, x1p, mod_p, ffn_g, ffn_u, ffn_d, 512, 256, seq)
            xs = _ffn(h2s, x1s, mod_s, ffn_g, ffn_u, ffn_d, n_s, 256, n_s)
        else:
            xp = _moe(h2p, x1p, mod_p, router_pad, moe_g, moe_u, moe_d, final_norm_g, 1024, 256, seq)
            xs = _moe(h2s, x1s, mod_s, router_pad, moe_g, moe_u, moe_d, final_norm_g, n_s, 256, n_s)

        hgrn_p.append(s_p)
        conv_p.append(c_p)
        hgrn_s.append(s_s)
        conv_s.append(c_s)

    return (xp.reshape(bsz, seq, D_MODEL), xs.reshape(n_s, 1, D_MODEL),
            jnp.stack(hgrn_p), jnp.stack(conv_p), jnp.stack(hgrn_s), jnp.stack(conv_s))
```

```python
import functools

import numpy as np
import jax
import jax.numpy as jnp
from jax import lax
from jax.experimental import pallas as pl
from jax.experimental.pallas import tpu as pltpu

F32 = jnp.float32
BF16 = jnp.bfloat16

D_MODEL = 1024
DEPTH = 2
HEADS = 8
HEAD_DIM = 128
CHUNK = 64
CONV_W = 31
N_EXPERTS = 8
D_FF = 2816
EPS = 1e-6
LOG2_E = 1.4426950408889634
LANES = 128
N_GROUPS_OUT = 7
VMEM_LIMIT = 48 * 1024 * 1024
MOE_VMEM_LIMIT = 56 * 1024 * 1024

LEVEL_WIDTHS = tuple(CHUNK >> (i + 1) for i in range(CHUNK.bit_length() - 1))


def _sigmoid(x):
    return 1.0 / (1.0 + jnp.exp(-x))


def _silu(x):
    return x * _sigmoid(x)


def _cparams(sem):
    return pltpu.CompilerParams(dimension_semantics=sem, vmem_limit_bytes=VMEM_LIMIT)


def _ada_kernel(c_ref, w_ref, b_ref, o_ref):
    a = _silu(c_ref[...]).astype(BF16)
    o_ref[...] = jnp.dot(a, w_ref[...].astype(BF16), preferred_element_type=F32) + b_ref[...]


def _ada(c_all, ada_w, ada_b):
    n = c_all.shape[0]
    tn = 1024
    return pl.pallas_call(
        _ada_kernel,
        grid=(DEPTH, 6 * D_MODEL // tn),
        in_specs=[
            pl.BlockSpec((n, D_MODEL), lambda l, j: (0, 0)),
            pl.BlockSpec((None, D_MODEL, tn), lambda l, j: (l, 0, j)),
            pl.BlockSpec((None, 1, tn), lambda l, j: (l, 0, j)),
        ],
        out_specs=pl.BlockSpec((None, n, tn), lambda l, j: (l, 0, j)),
        out_shape=jax.ShapeDtypeStruct((DEPTH, n, 6 * D_MODEL), F32),
        compiler_params=_cparams(("parallel", "parallel")),
        name="ada",
    )(c_all, ada_w, ada_b.reshape(DEPTH, 1, 6 * D_MODEL))


def _inproj_kernel(x_ref, sh_ref, sc_ref, g_ref, lbl_ref, w_ref, z_ref, lf_ref, h_scr, *, layer):
    j = pl.program_id(1)

    @pl.when(j == 0)
    def _():
        x = x_ref[...]
        y = x * lax.rsqrt(jnp.mean(x * x, axis=-1, keepdims=True) + EPS) * g_ref[...]
        h_scr[...] = (y * (1.0 + sc_ref[...]) + sh_ref[...]).astype(BF16)

    z = jnp.dot(h_scr[...], w_ref[...], preferred_element_type=F32)

    @pl.when(j == 0)
    def _():
        z_ref[...] = _silu(z)

    @pl.when(j == 1)
    def _():
        lbl = lbl_ref[...]
        e = jnp.exp(lbl - jnp.max(lbl, axis=0, keepdims=True))
        p = e / jnp.sum(e, axis=0, keepdims=True)
        cum = p[0:1]
        for i in range(1, layer + 1):
            cum = cum + p[i:i + 1]
        lb = cum - p[0:1]
        z_ref[...] = (1.0 - lb) * _sigmoid(-z)
        lf_ref[...] = jnp.log(lb + (1.0 - lb) * _sigmoid(z))

    @pl.when(j == 2)
    def _():
        z_ref[...] = z

    @pl.when(j == 3)
    def _():
        z_ref[...] = _silu(z)

    @pl.when(j == 4)
    def _():
        z_ref[...] = z

    @pl.when(j == 5)
    def _():
        z_ref[...] = z_ref[...] * _sigmoid(z)

    @pl.when(j >= 6)
    def _():
        z_ref[...] = _sigmoid(z)


def _inproj(x, mod, norm_g, lb_logits, w_in_b, layer, tm, rows_per_mod):
    m = x.shape[0]
    mod_rows = mod.shape[1]

    def mod_spec(col):
        return pl.BlockSpec((None, mod_rows, D_MODEL),
                            lambda i, j: ((i * tm) // rows_per_mod, 0, col))

    return pl.pallas_call(
        functools.partial(_inproj_kernel, layer=layer),
        grid=(m // tm, 8),
        in_specs=[
            pl.BlockSpec((tm, D_MODEL), lambda i, j: (i, 0)),
            mod_spec(0), mod_spec(1),
            pl.BlockSpec((1, D_MODEL), lambda i, j: (0, 0)),
            pl.BlockSpec((DEPTH, D_MODEL), lambda i, j: (0, 0)),
            pl.BlockSpec((None, D_MODEL, D_MODEL), lambda i, j: (layer, 0, j)),
        ],
        out_specs=[
            pl.BlockSpec((tm, D_MODEL), lambda i, j: (i, jnp.where(j >= 5, j - 1, j))),
            pl.BlockSpec((tm, D_MODEL), lambda i, j: (i, 0)),
        ],
        out_shape=[
            jax.ShapeDtypeStruct((m, N_GROUPS_OUT * D_MODEL), F32),
            jax.ShapeDtypeStruct((m, D_MODEL), F32),
        ],
        scratch_shapes=[pltpu.VMEM((tm, D_MODEL), BF16)],
        compiler_params=_cparams(("parallel", "arbitrary")),
        name="inproj",
    )(x, mod, mod, norm_g.reshape(1, D_MODEL), lb_logits, w_in_b)


def _hgrn_constants():
    t = np.arange(CHUNK)
    s = t[None, :]
    blocks = [(s <= t[:, None])]
    for w in LEVEL_WIDTHS:
        ref_row = ((t & ~(2 * w - 1)) + w - 1)[:, None]
        is_query = ((t & w) != 0)[:, None]
        blocks.append(np.where(is_query, (s > ref_row) & (s <= t[:, None]),
                               (s > t[:, None]) & (s <= ref_row)))
    cum = np.concatenate(blocks, axis=0).astype(np.float32)
    acat = np.concatenate([cum, cum, cum, np.zeros_like(cum)], axis=1)
    x = t[:, None] ^ t[None, :]
    lvl = np.full((CHUNK, CHUNK), -1, np.int32)
    for li, w in enumerate(LEVEL_WIDTHS):
        lvl[((x // w) == 1) & ((t[:, None] & w) != 0)] = li
    lvl[t[:, None] == t[None, :]] = len(LEVEL_WIDTHS)
    return jnp.asarray(acat, BF16), jnp.asarray(np.tile(lvl, (1, 2)))


def _block_diag(a, b):
    z = jnp.zeros_like(a)
    return jnp.concatenate([jnp.concatenate([a, z], axis=1),
                            jnp.concatenate([z, b], axis=1)], axis=0)


def _hgrn_prompt_kernel(q_ref, k_ref, v_ref, og_ref, lf_ref, gn_ref, acat_ref, lvl_ref,
                        o_ref, s_ref, st_scr, *, n_chunks):
    tb = pl.program_id(1)
    seqs = range(st_scr.shape[0])

    @pl.when(tb == 0)
    def _():
        st_scr[...] = jnp.zeros_like(st_scr)

    lvl = lvl_ref[...]
    trans_b = (((1,), (1,)), ((), ()))
    trans_a = (((0,), (0,)), ((), ()))
    pair_w = 2 * HEAD_DIM
    n_pairs = HEADS // 2
    n_lvl = len(LEVEL_WIDTHS)

    def head(x, h):
        return x[:, h * HEAD_DIM:(h + 1) * HEAD_DIM]

    def pair_rows(x, p):
        return _block_diag(head(x, 2 * p), head(x, 2 * p + 1))

    def lanes(p):
        return slice(p * pair_w, (p + 1) * pair_w)

    def chunk_body(c, carry):
        rows = pl.ds(pl.multiple_of(c * CHUNK, CHUNK), CHUNK)
        q = [q_ref[s, rows, :] for s in seqs]
        k = [k_ref[s, rows, :] for s in seqs]
        v = [v_ref[s, rows, :].astype(BF16) for s in seqs]
        cums = []
        for s in seqs:
            lf = lf_ref[s, rows, :] * LOG2_E
            hi = lf.astype(BF16)
            r1 = lf - hi.astype(F32)
            mid = r1.astype(BF16)
            lo = (r1 - mid.astype(F32)).astype(BF16)
            lf4 = jnp.concatenate([hi, mid, lo, jnp.zeros_like(hi)], axis=0)
            cums.append(jnp.dot(acat_ref[...], lf4, preferred_element_type=F32))
        qb = [q[s].astype(BF16) for s in seqs]
        kb = [k[s].astype(BF16) for s in seqs]
        att = [[jnp.where(lvl == n_lvl,
                          lax.dot_general(qb[s][:, lanes(p)], pair_rows(kb[s], p), trans_b,
                                          preferred_element_type=F32), 0.0)
                for p in range(n_pairs)] for s in seqs]
        for li in range(n_lvl):
            for s in seqs:
                e = jnp.exp2(cums[s][(li + 1) * CHUNK:(li + 2) * CHUNK]).astype(BF16)
                xq = qb[s] * e
                xk = kb[s] * e
                for p in range(n_pairs):
                    g = lax.dot_general(xq[:, lanes(p)], pair_rows(xk, p), trans_b,
                                        preferred_element_type=F32)
                    att[s][p] = jnp.where(lvl == li, g, att[s][p])
        inter = []
        for s in seqs:
            b = cums[s][0:CHUNK]
            qe = (q[s] * jnp.exp2(b)).astype(BF16)
            inter.append([lax.dot_general(
                qe[:, lanes(p)],
                _block_diag(st_scr[s, 2 * p].astype(BF16), st_scr[s, 2 * p + 1].astype(BF16)),
                trans_b, preferred_element_type=F32) for p in range(n_pairs)])
        for s in seqs:
            b = cums[s][0:CHUNK]
            b_last = b[CHUNK - 1:CHUNK]
            kd = (k[s] * jnp.exp2(b_last - b)).astype(BF16)
            decay = jnp.exp2(b_last)
            for h in range(HEADS):
                st_scr[s, h] = head(decay, h) * st_scr[s, h] + lax.dot_general(
                    head(v[s], h), head(kd, h), trans_a, preferred_element_type=F32)
        for s in seqs:
            for p in range(n_pairs):
                o_pair = inter[s][p] + jnp.dot(att[s][p].astype(BF16), pair_rows(v[s], p),
                                               preferred_element_type=F32)
                for h in (2 * p, 2 * p + 1):
                    sl = slice(h * HEAD_DIM, (h + 1) * HEAD_DIM)
                    o = head(o_pair, h % 2)
                    o = o * lax.rsqrt(jnp.mean(o * o, axis=-1, keepdims=True) + EPS)
                    o = o * gn_ref[:, sl] * og_ref[s, rows, sl]
                    o_ref[s, rows, sl] = o.astype(BF16)
        return carry

    lax.fori_loop(0, n_chunks, chunk_body, 0)

    @pl.when(tb == pl.num_programs(1) - 1)
    def _():
        for s in seqs:
            for h in range(HEADS):
                s_ref[s, h] = st_scr[s, h].T


def _hgrn_prompt(z3, lf3, gn, acat, lvl, tb, nb):
    bsz, t, _ = z3.shape

    def zspec(col):
        return pl.BlockSpec((nb, tb, D_MODEL), lambda b, i: (b, i, col))

    return pl.pallas_call(
        functools.partial(_hgrn_prompt_kernel, n_chunks=tb // CHUNK),
        grid=(bsz // nb, t // tb),
        in_specs=[
            zspec(0), zspec(1), zspec(2), zspec(3),
            pl.BlockSpec((nb, tb, D_MODEL), lambda b, i: (b, i, 0)),
            pl.BlockSpec((1, D_MODEL), lambda b, i: (0, 0)),
            pl.BlockSpec(acat.shape, lambda b, i: (0, 0)),
            pl.BlockSpec(lvl.shape, lambda b, i: (0, 0)),
        ],
        out_specs=[
            pl.BlockSpec((nb, tb, D_MODEL), lambda b, i: (b, i, 0)),
            pl.BlockSpec((nb, HEADS, HEAD_DIM, HEAD_DIM), lambda b, i: (b, 0, 0, 0)),
        ],
        out_shape=[
            jax.ShapeDtypeStruct((bsz, t, D_MODEL), BF16),
            jax.ShapeDtypeStruct((bsz, HEADS, HEAD_DIM, HEAD_DIM), F32),
        ],
        scratch_shapes=[pltpu.VMEM((nb, HEADS, HEAD_DIM, HEAD_DIM), F32)],
        compiler_params=_cparams(("parallel", "arbitrary")),
        name="hgrn_prompt",
    )(z3, z3, z3, z3, lf3, gn.reshape(1, D_MODEL), acat, lvl)


def _hgrn_sample_kernel(q_ref, k_ref, v_ref, og_ref, lf_ref, gn_ref, s_ref, o_ref, so_ref, o_scr,
                        *, bb):
    n = bb * HEADS
    qt = q_ref[...].reshape(n, HEAD_DIM).T
    kt = k_ref[...].reshape(n, HEAD_DIM).T
    ft = jnp.exp(lf_ref[...].reshape(n, HEAD_DIM)).T
    for b in range(bb):
        for h in range(HEADS):
            c = b * HEADS + h
            s_new = ft[:, c:c + 1] * s_ref[b, h] + kt[:, c:c + 1] * v_ref[b, h:h + 1, :]
            so_ref[b, h] = s_new
            o_scr[c:c + 1, :] = jnp.sum(qt[:, c:c + 1] * s_new, axis=0, keepdims=True)
    o = o_scr[...]
    o = o * lax.rsqrt(jnp.mean(o * o, axis=-1, keepdims=True) + EPS)
    o = o * gn_ref[...] * og_ref[...].reshape(n, HEAD_DIM)
    o_ref[...] = o.astype(BF16)


def _hgrn_sample(z, lf, gn, state, bb):
    n = z.shape[0]
    z3 = z.reshape(n, N_GROUPS_OUT * HEADS, HEAD_DIM)
    lf3 = lf.reshape(n, HEADS, HEAD_DIM)
    gn_rows = jnp.tile(gn.reshape(HEADS, HEAD_DIM), (bb, 1))

    def zspec(col):
        return pl.BlockSpec((bb, HEADS, HEAD_DIM), lambda i: (i, col, 0))

    state_spec = pl.BlockSpec((bb, HEADS, HEAD_DIM, HEAD_DIM), lambda i: (i, 0, 0, 0))
    o, s_new = pl.pallas_call(
        functools.partial(_hgrn_sample_kernel, bb=bb),
        grid=(n // bb,),
        in_specs=[
            zspec(0), zspec(1), zspec(2), zspec(3),
            pl.BlockSpec((bb, HEADS, HEAD_DIM), lambda i: (i, 0, 0)),
            pl.BlockSpec((bb * HEADS, HEAD_DIM), lambda i: (0, 0)),
            state_spec,
        ],
        out_specs=[
            pl.BlockSpec((bb * HEADS, HEAD_DIM), lambda i: (i, 0)),
            state_spec,
        ],
        out_shape=[
            jax.ShapeDtypeStruct((n * HEADS, HEAD_DIM), BF16),
            jax.ShapeDtypeStruct(state.shape, F32),
        ],
        scratch_shapes=[pltpu.VMEM((bb * HEADS, HEAD_DIM), F32)],
        compiler_params=_cparams(("parallel",)),
        name="hgrn_sample",
    )(z3, z3, z3, z3, lf3, gn_rows, state)
    return o.reshape(n, D_MODEL), s_new


CONV_HIST = 32
CONV_ROWS = 16


def _conv_prompt_kernel(u_ref, wb_ref, bias_ref, y_ref, buf, shifted, *, tc):
    @pl.when(pl.program_id(1) == 0)
    def _():
        buf[0:CONV_HIST, :] = jnp.zeros((CONV_HIST, D_MODEL), F32)

    buf[CONV_HIST:CONV_HIST + tc, :] = u_ref[...]
    lead = CONV_HIST - (CONV_W - 1)
    n_shift = shifted.shape[1]
    for s in range(1, 8):
        shifted[s - 1] = buf[s:s + n_shift, :]

    def rows_body(r, carry):
        base = pl.multiple_of(r * CONV_ROWS, CONV_ROWS)
        accs = [jnp.broadcast_to(bias_ref[...], (8, D_MODEL)) for _ in range(CONV_ROWS // 8)]
        for j in range(CONV_W):
            wj = wb_ref[j * 8:(j + 1) * 8, :]
            p, s = divmod(lead + j, 8)
            for a in range(CONV_ROWS // 8):
                rows = pl.ds(base + 8 * (a + p), 8)
                tap = buf[rows, :] if s == 0 else shifted[s - 1, rows, :]
                accs[a] = accs[a] + wj * tap
        for a in range(CONV_ROWS // 8):
            y_ref[pl.ds(base + a * 8, 8), :] = accs[a]
        return carry

    lax.fori_loop(0, tc // CONV_ROWS, rows_body, 0)
    buf[0:CONV_HIST, :] = buf[tc:tc + CONV_HIST, :]


def _conv_prompt(z3, conv_w, conv_b, tc):
    bsz, t, _ = z3.shape
    wb = jnp.repeat(conv_w, 8, axis=0)
    return pl.pallas_call(
        functools.partial(_conv_prompt_kernel, tc=tc),
        grid=(bsz, t // tc),
        in_specs=[
            pl.BlockSpec((None, tc, D_MODEL), lambda b, i: (b, i, 4)),
            pl.BlockSpec((CONV_W * 8, D_MODEL), lambda b, i: (0, 0)),
            pl.BlockSpec((1, D_MODEL), lambda b, i: (0, 0)),
        ],
        out_specs=pl.BlockSpec((None, tc, D_MODEL), lambda b, i: (b, i, 0)),
        out_shape=jax.ShapeDtypeStruct((bsz, t, D_MODEL), F32),
        scratch_shapes=[pltpu.VMEM((CONV_HIST + tc, D_MODEL), F32),
                        pltpu.VMEM((7, CONV_HIST + tc - 8, D_MODEL), F32)],
        compiler_params=_cparams(("parallel", "arbitrary")),
        name="conv_prompt",
    )(z3, wb, conv_b.reshape(1, D_MODEL))


def _conv_sample_kernel(u_ref, s_ref, w_ref, bias_ref, y_ref, so_ref, *, bb):
    hist = CONV_W - 1
    for b in range(bb):
        u = u_ref[b:b + 1, :]
        y = jnp.sum(s_ref[b] * w_ref[0:hist, :], axis=0, keepdims=True)
        y_ref[b:b + 1, :] = y + w_ref[hist:CONV_W, :] * u + bias_ref[...]
        so_ref[b, 0:hist - 1, :] = s_ref[b, 1:hist, :]
        so_ref[b, hist - 1:hist, :] = u


def _conv_sample(z, state, conv_w, conv_b, bb):
    n = z.shape[0]
    hist = CONV_W - 1
    state_spec = pl.BlockSpec((bb, hist, D_MODEL), lambda i: (i, 0, 0))
    return pl.pallas_call(
        functools.partial(_conv_sample_kernel, bb=bb),
        grid=(n // bb,),
        in_specs=[
            pl.BlockSpec((bb, D_MODEL), lambda i: (i, 4)),
            state_spec,
            pl.BlockSpec((CONV_W, D_MODEL), lambda i: (0, 0)),
            pl.BlockSpec((1, D_MODEL), lambda i: (0, 0)),
        ],
        out_specs=[pl.BlockSpec((bb, D_MODEL), lambda i: (i, 0)), state_spec],
        out_shape=[jax.ShapeDtypeStruct((n, D_MODEL), F32),
                   jax.ShapeDtypeStruct(state.shape, F32)],
        compiler_params=_cparams(("parallel",)),
        name="conv_sample",
    )(z, state, conv_w, conv_b.reshape(1, D_MODEL))


def _merge_kernel(yc_ref, oa_ref, ga_ref, gb_ref, x_ref, g1_ref, sh2_ref, sc2_ref,
                  lng_ref, lnb_ref, n2g_ref, wco_ref, whg_ref, wo_ref, x1_ref, h2_ref):
    yc = yc_ref[...]
    xc = yc - jnp.mean(yc, axis=-1, keepdims=True)
    vln = xc * lax.rsqrt(jnp.mean(xc * xc, axis=-1, keepdims=True) + EPS)
    va = _silu(vln * lng_ref[...] + lnb_ref[...]).astype(BF16)
    y_b = jnp.dot(va, wco_ref[...], preferred_element_type=F32)
    y_a = jnp.dot(oa_ref[...], whg_ref[...], preferred_element_type=F32)
    mixed = (ga_ref[...] * y_a + gb_ref[...] * y_b).astype(BF16)
    x1 = x_ref[...] + g1_ref[...] * jnp.dot(mixed, wo_ref[...], preferred_element_type=F32)
    x1_ref[...] = x1
    y = x1 * lax.rsqrt(jnp.mean(x1 * x1, axis=-1, keepdims=True) + EPS) * n2g_ref[...]
    h2_ref[...] = (y * (1.0 + sc2_ref[...]) + sh2_ref[...]).astype(h2_ref.dtype)


def _merge(yc, oa, z, x, mod, ln_g, ln_b, n2_g, wco, whg, wo, layer, tm, rows_per_mod,
           h2_dtype=BF16):
    m = x.shape[0]
    mod_rows = mod.shape[1]

    def mod_spec(col):
        return pl.BlockSpec((None, mod_rows, D_MODEL),
                            lambda i: ((i * tm) // rows_per_mod, 0, col))

    def row_spec(col=0):
        return pl.BlockSpec((tm, D_MODEL), lambda i: (i, col))

    vec_spec = pl.BlockSpec((1, D_MODEL), lambda i: (0, 0))
    w_spec = pl.BlockSpec((None, D_MODEL, D_MODEL), lambda i: (layer, 0, 0))
    return pl.pallas_call(
        _merge_kernel,
        grid=(m // tm,),
        in_specs=[row_spec(), row_spec(), row_spec(5), row_spec(6), row_spec(),
                  mod_spec(2), mod_spec(3), mod_spec(4),
                  vec_spec, vec_spec, vec_spec, w_spec, w_spec, w_spec],
        out_specs=[row_spec(), row_spec()],
        out_shape=[jax.ShapeDtypeStruct((m, D_MODEL), F32),
                   jax.ShapeDtypeStruct((m, D_MODEL), h2_dtype)],
        compiler_params=_cparams(("parallel",)),
        name="merge",
    )(yc, oa, z, z, x, mod, mod, mod,
      ln_g.reshape(1, D_MODEL), ln_b.reshape(1, D_MODEL), n2_g.reshape(1, D_MODEL),
      wco, whg, wo)


def _ffn_kernel(h_ref, x1_ref, g2_ref, wg_ref, wu_ref, wd_ref, o_ref, acc):
    j = pl.program_id(1)

    @pl.when(j == 0)
    def _():
        acc[...] = jnp.zeros_like(acc)

    h = h_ref[...]
    a = _silu(jnp.dot(h, wg_ref[...], preferred_element_type=F32))
    a = a * jnp.dot(h, wu_ref[...], preferred_element_type=F32)
    acc[...] += jnp.dot(a.astype(BF16), wd_ref[...], preferred_element_type=F32)

    @pl.when(j == pl.num_programs(1) - 1)
    def _():
        o_ref[...] = x1_ref[...] + g2_ref[...] * acc[...]


def _ffn(h2, x1, mod, wg, wu, wd, tm, tf, rows_per_mod):
    m = x1.shape[0]
    mod_rows = mod.shape[1]
    return pl.pallas_call(
        _ffn_kernel,
        grid=(m // tm, D_FF // tf),
        in_specs=[
            pl.BlockSpec((tm, D_MODEL), lambda i, j: (i, 0)),
            pl.BlockSpec((tm, D_MODEL), lambda i, j: (i, 0)),
            pl.BlockSpec((None, mod_rows, D_MODEL),
                         lambda i, j: ((i * tm) // rows_per_mod, 0, 5)),
            pl.BlockSpec((None, D_MODEL, tf), lambda i, j: (0, 0, j)),
            pl.BlockSpec((None, D_MODEL, tf), lambda i, j: (0, 0, j)),
            pl.BlockSpec((None, tf, D_MODEL), lambda i, j: (0, j, 0)),
        ],
        out_specs=pl.BlockSpec((tm, D_MODEL), lambda i, j: (i, 0)),
        out_shape=jax.ShapeDtypeStruct((m, D_MODEL), F32),
        scratch_shapes=[pltpu.VMEM((tm, D_MODEL), F32)],
        compiler_params=_cparams(("parallel", "arbitrary")),
        name="ffn",
    )(h2, x1, mod, wg, wu, wd)


def _top2(logits, lane):
    neg = jnp.float32(-jnp.inf)
    logits = jnp.where(lane < N_EXPERTS, logits, neg)
    v1 = jnp.max(logits, axis=-1, keepdims=True)
    i1 = jnp.min(jnp.where(logits == v1, lane, LANES), axis=-1, keepdims=True)
    rest = jnp.where(lane == i1, neg, logits)
    v2 = jnp.max(rest, axis=-1, keepdims=True)
    i2 = jnp.min(jnp.where(rest == v2, lane, LANES), axis=-1, keepdims=True)
    e2 = jnp.exp(v2 - v1)
    w1 = 1.0 / (1.0 + e2)
    return i1, i2, w1, e2 * w1


def _route_kernel(h_ref, rw_ref, r_ref):
    lane = lax.broadcasted_iota(jnp.int32, r_ref.shape, 1)
    logits = jnp.dot(h_ref[...].astype(BF16), rw_ref[...], preferred_element_type=F32)
    i1, i2, w1, w2 = _top2(logits, lane)
    r_ref[...] = (jnp.where(lane == 0, i1.astype(F32), 0.0) + jnp.where(lane == 1, i2.astype(F32), 0.0)
                  + jnp.where(lane == 2, w1, 0.0) + jnp.where(lane == 3, w2, 0.0))


def _route(h2, router_pad, tm):
    m = h2.shape[0]
    return pl.pallas_call(
        _route_kernel,
        grid=(m // tm,),
        in_specs=[pl.BlockSpec((tm, D_MODEL), lambda i: (i, 0)),
                  pl.BlockSpec((D_MODEL, LANES), lambda i: (0, 0))],
        out_specs=pl.BlockSpec((tm, LANES), lambda i: (i, 0)),
        out_shape=jax.ShapeDtypeStruct((m, LANES), F32),
        compiler_params=_cparams(("parallel",)),
        name="route",
    )(h2, router_pad)


def _start_row_gather(idx_ref, n_rows, src_hbm, dst, sem):
    def body(r, carry):
        pltpu.make_async_copy(src_hbm.at[pl.ds(idx_ref[r], 1), :], dst.at[pl.ds(r, 1), :], sem).start()
        return carry
    lax.fori_loop(0, n_rows, body, 0)


def _moe_sparse_kernel(te_ref, na_ref, idx_ref, idx_next_ref, h_hbm, wg_ref, wu_ref, wd_ref,
                       y_ref, buf, sem, *, tf):
    i = pl.program_id(0)
    n_active = na_ref[0]
    slot = lax.rem(i, 2)
    tm = buf.shape[1]

    @pl.when(i == 0)
    def _():
        _start_row_gather(idx_ref.at[0], tm, h_hbm, buf.at[0], sem.at[0])

    @pl.when(i + 1 < n_active)
    def _():
        _start_row_gather(idx_next_ref.at[0], tm, h_hbm, buf.at[1 - slot], sem.at[1 - slot])

    @pl.when(i < n_active)
    def _():
        pltpu.make_async_copy(buf.at[slot], buf.at[slot], sem.at[slot]).wait()
        x = buf[slot].astype(BF16)
        acc = jnp.zeros((tm, D_MODEL), F32)
        for j in range(D_FF // tf):
            cols = slice(j * tf, (j + 1) * tf)
            a = _silu(jnp.dot(x, wg_ref[:, cols], preferred_element_type=F32))
            a = a * jnp.dot(x, wu_ref[:, cols], preferred_element_type=F32)
            acc = acc + jnp.dot(a.astype(BF16), wd_ref[cols, :], preferred_element_type=F32)
        y_ref[...] = acc

    @pl.when(i >= n_active)
    def _():
        y_ref[...] = jnp.zeros_like(y_ref)


def _moe_sparse(h2, src_tok, tile_expert, n_active, wg, wu, wd, tm, tf):
    n_tiles = src_tok.shape[0]
    idx_spec = lambda f: pl.BlockSpec((None, 1, tm), f, memory_space=pltpu.SMEM)
    w_up_spec = pl.BlockSpec((None, None, D_MODEL, D_FF), lambda i, te, na: (0, te[i], 0, 0))
    grid_spec = pltpu.PrefetchScalarGridSpec(
        num_scalar_prefetch=2,
        grid=(n_tiles,),
        in_specs=[
            idx_spec(lambda i, te, na: (i, 0, 0)),
            idx_spec(lambda i, te, na: (jnp.minimum(i + 1, n_tiles - 1), 0, 0)),
            pl.BlockSpec(memory_space=pl.ANY),
            w_up_spec, w_up_spec,
            pl.BlockSpec((None, None, D_FF, D_MODEL), lambda i, te, na: (0, te[i], 0, 0)),
        ],
        out_specs=pl.BlockSpec((tm, D_MODEL), lambda i, te, na: (i, 0)),
        scratch_shapes=[pltpu.VMEM((2, tm, D_MODEL), F32), pltpu.SemaphoreType.DMA((2,))],
    )
    return pl.pallas_call(
        functools.partial(_moe_sparse_kernel, tf=tf),
        grid_spec=grid_spec,
        out_shape=jax.ShapeDtypeStruct((n_tiles * tm, D_MODEL), F32),
        compiler_params=pltpu.CompilerParams(dimension_semantics=("arbitrary",),
                                             vmem_limit_bytes=MOE_VMEM_LIMIT),
        name="moe_sparse",
    )(tile_expert, n_active, src_tok, src_tok, h2, wg, wu, wd)


def _moe_combine_kernel(pos_ref, pos_next_ref, y_hbm, r_ref, x1_ref, g2_ref, fg_ref, o_ref,
                        buf, sem):
    i = pl.program_id(0)
    slot = lax.rem(i, 2)
    n_rows = buf.shape[1]

    @pl.when(i == 0)
    def _():
        _start_row_gather(pos_ref.at[0], n_rows, y_hbm, buf.at[0], sem.at[0])

    @pl.when(i + 1 < pl.num_programs(0))
    def _():
        _start_row_gather(pos_next_ref.at[0], n_rows, y_hbm, buf.at[1 - slot], sem.at[1 - slot])

    pltpu.make_async_copy(buf.at[slot], buf.at[slot], sem.at[slot]).wait()
    tc = n_rows // 2
    lane = lax.broadcasted_iota(jnp.int32, r_ref.shape, 1)
    r = r_ref[...]
    w1 = jnp.sum(jnp.where(lane == 2, r, 0.0), axis=-1, keepdims=True)
    w2 = jnp.sum(jnp.where(lane == 3, r, 0.0), axis=-1, keepdims=True)
    f = w1 * buf[slot, 0:tc, :] + w2 * buf[slot, tc:n_rows, :]
    x2 = x1_ref[...] + g2_ref[...] * f
    o_ref[...] = x2 * lax.rsqrt(jnp.mean(x2 * x2, axis=-1, keepdims=True) + EPS) * fg_ref[...]


def _moe_combine(y_sorted, pos, route, x1, mod, final_g, tc, rows_per_mod):
    m = x1.shape[0]
    n_blocks = m // tc
    mod_rows = mod.shape[1]
    pos_spec = lambda f: pl.BlockSpec((None, 1, 2 * tc), f, memory_space=pltpu.SMEM)
    return pl.pallas_call(
        _moe_combine_kernel,
        grid=(n_blocks,),
        in_specs=[
            pos_spec(lambda i: (i, 0, 0)),
            pos_spec(lambda i: (jnp.minimum(i + 1, n_blocks - 1), 0, 0)),
            pl.BlockSpec(memory_space=pl.ANY),
            pl.BlockSpec((tc, LANES), lambda i: (i, 0)),
            pl.BlockSpec((tc, D_MODEL), lambda i: (i, 0)),
            pl.BlockSpec((None, mod_rows, D_MODEL), lambda i: ((i * tc) // rows_per_mod, 0, 5)),
            pl.BlockSpec((1, D_MODEL), lambda i: (0, 0)),
        ],
        out_specs=pl.BlockSpec((tc, D_MODEL), lambda i: (i, 0)),
        out_shape=jax.ShapeDtypeStruct((m, D_MODEL), F32),
        scratch_shapes=[pltpu.VMEM((2, 2 * tc, D_MODEL), F32), pltpu.SemaphoreType.DMA((2,))],
        compiler_params=_cparams(("arbitrary",)),
        name="moe_combine",
    )(pos, pos, y_sorted, route, x1, mod, final_g.reshape(1, D_MODEL))


def _moe_routed(h2, x1, mod, router_pad, wg, wu, wd, final_g, tm, tf, tc, rows_per_mod):
    n = h2.shape[0]
    route = _route(h2, router_pad, 512)
    expert = route[:, :2].astype(jnp.int32).reshape(2 * n)
    onehot = (expert[:, None] == jnp.arange(N_EXPERTS, dtype=jnp.int32)[None, :]).astype(jnp.int32)
    running = jnp.cumsum(onehot, axis=0)
    rank = jnp.sum((running - onehot) * onehot, axis=1)
    padded = ((running[-1] + tm - 1) // tm) * tm
    group_end = jnp.cumsum(padded)
    pos = (group_end - padded)[expert] + rank
    n_tiles = (2 * n) // tm + N_EXPERTS
    src_tok = jnp.zeros((n_tiles * tm,), jnp.int32).at[pos].set(
        jnp.arange(2 * n, dtype=jnp.int32) // 2)
    tile_start = jnp.arange(n_tiles, dtype=jnp.int32) * tm
    tile_expert = jnp.minimum(jnp.sum(tile_start[:, None] >= group_end[None, :], axis=1),
                              N_EXPERTS - 1).astype(jnp.int32)
    n_active = (group_end[-1:] // tm).astype(jnp.int32)

    y_sorted = _moe_sparse(h2, src_tok.reshape(n_tiles, 1, tm), tile_expert, n_active,
                           wg, wu, wd, tm, tf)
    pos_blocks = pos.reshape(n // tc, tc, 2).transpose(0, 2, 1).reshape(n // tc, 1, 2 * tc)
    return _moe_combine(y_sorted, pos_blocks, route, x1, mod, final_g, tc, rows_per_mod)


def _moe_kernel(h_ref, x1_ref, g2_ref, rw_ref, wg_ref, wu_ref, wd_ref, fg_ref, o_ref,
                acc, comb):
    e = pl.program_id(1)
    j = pl.program_id(2)
    lane = lax.broadcasted_iota(jnp.int32, comb.shape, 1)

    @pl.when((e == 0) & (j == 0))
    def _():
        acc[...] = jnp.zeros_like(acc)
        i1, i2, w1, w2 = _top2(jnp.dot(h_ref[...], rw_ref[...], preferred_element_type=F32), lane)
        comb[...] = jnp.where(lane == i1, w1, 0.0) + jnp.where(lane == i2, w2, 0.0)

    ce = jnp.sum(jnp.where(lane == e, comb[...], 0.0), axis=-1, keepdims=True)
    h = h_ref[...]
    a = _silu(jnp.dot(h, wg_ref[...], preferred_element_type=F32))
    a = a * jnp.dot(h, wu_ref[...], preferred_element_type=F32) * ce
    acc[...] += jnp.dot(a.astype(BF16), wd_ref[...], preferred_element_type=F32)

    @pl.when((e == pl.num_programs(1) - 1) & (j == pl.num_programs(2) - 1))
    def _():
        x2 = x1_ref[...] + g2_ref[...] * acc[...]
        o_ref[...] = x2 * lax.rsqrt(jnp.mean(x2 * x2, axis=-1, keepdims=True) + EPS) * fg_ref[...]


def _moe(h2, x1, mod, router_pad, wg, wu, wd, final_g, tm, tf, rows_per_mod):
    m = x1.shape[0]
    mod_rows = mod.shape[1]
    return pl.pallas_call(
        _moe_kernel,
        grid=(m // tm, N_EXPERTS, D_FF // tf),
        in_specs=[
            pl.BlockSpec((tm, D_MODEL), lambda i, e, j: (i, 0)),
            pl.BlockSpec((tm, D_MODEL), lambda i, e, j: (i, 0)),
            pl.BlockSpec((None, mod_rows, D_MODEL),
                         lambda i, e, j: ((i * tm) // rows_per_mod, 0, 5)),
            pl.BlockSpec((D_MODEL, LANES), lambda i, e, j: (0, 0)),
            pl.BlockSpec((None, None, D_MODEL, tf), lambda i, e, j: (0, e, 0, j)),
            pl.BlockSpec((None, None, D_MODEL, tf), lambda i, e, j: (0, e, 0, j)),
            pl.BlockSpec((None, None, tf, D_MODEL), lambda i, e, j: (0, e, j, 0)),
            pl.BlockSpec((1, D_MODEL), lambda i, e, j: (0, 0)),
        ],
        out_specs=pl.BlockSpec((tm, D_MODEL), lambda i, e, j: (i, 0)),
        out_shape=jax.ShapeDtypeStruct((m, D_MODEL), F32),
        scratch_shapes=[pltpu.VMEM((tm, D_MODEL), F32), pltpu.VMEM((tm, LANES), F32)],
        compiler_params=_cparams(("parallel", "arbitrary", "arbitrary")),
        name="moe",
    )(h2, x1, mod, router_pad, wg, wu, wd, final_g.reshape(1, D_MODEL))


def kernel(x_prompt, x_sample, state_hgrn, state_conv, c_prompt, c_sample, ada_w, ada_b, norm1_g, norm2_g, w_in, lb_logits, hg_norm_g, w_hg_out, conv_w, conv_b, conv_ln_g, conv_ln_b, w_conv_out, w_o, ffn_w_gate, ffn_w_up, ffn_w_down, router_w, moe_w_gate, moe_w_up, moe_w_down, final_norm_g):
    bsz, seq, _ = x_prompt.shape
    n_s = x_sample.shape[0]
    hist = CONV_W - 1

    w_in_b = w_in.astype(BF16)
    whg_b = w_hg_out.astype(BF16)
    wco_b = w_conv_out.astype(BF16)
    wo_b = w_o.astype(BF16)
    ffn_g, ffn_u, ffn_d = (w.astype(BF16) for w in (ffn_w_gate, ffn_w_up, ffn_w_down))
    moe_g, moe_u, moe_d = (w.astype(BF16) for w in (moe_w_gate, moe_w_up, moe_w_down))
    router_pad = jnp.pad(router_w[0], ((0, 0), (0, LANES - N_EXPERTS))).astype(BF16)
    acat, lvl = _hgrn_constants()

    mod_all = _ada(jnp.concatenate([c_prompt, c_sample], axis=0), ada_w, ada_b)

    xp = x_prompt.reshape(bsz * seq, D_MODEL)
    xs = x_sample.reshape(n_s, D_MODEL)
    tm_p = 512
    hgrn_p, conv_p, hgrn_s, conv_s = [], [], [], []
    for l in range(DEPTH):
        mod_p = mod_all[l, :bsz].reshape(bsz, 1, 6 * D_MODEL)
        mod_s = mod_all[l, bsz:].reshape(1, n_s, 6 * D_MODEL)

        zp, lfp = _inproj(xp, mod_p, norm1_g[l], lb_logits, w_in_b, l, tm_p, seq)
        zs, lfs = _inproj(xs, mod_s, norm1_g[l], lb_logits, w_in_b, l, n_s, n_s)

        zp3 = zp.reshape(bsz, seq, N_GROUPS_OUT * D_MODEL)
        oa_p, s_p = _hgrn_prompt(zp3, lfp.reshape(bsz, seq, D_MODEL), hg_norm_g[l], acat, lvl, 256, 2)
        oa_s, s_s = _hgrn_sample(zs, lfs, hg_norm_g[l], state_hgrn[l], 8)

        yc_p = _conv_prompt(zp3, conv_w[l], conv_b[l], 256)
        yc_s, c_s = _conv_sample(zs, state_conv[l], conv_w[l], conv_b[l], 8)
        c_p = zp3[:, seq - hist:, 4 * D_MODEL:5 * D_MODEL]

        merge_w = (conv_ln_g[l], conv_ln_b[l], norm2_g[l], wco_b, whg_b, wo_b, l)
        dense = l % 2 == 0
        x1p, h2p = _merge(yc_p.reshape(bsz * seq, D_MODEL), oa_p.reshape(bsz * seq, D_MODEL),
                          zp, xp, mod_p, *merge_w, 256, seq, BF16 if dense else F32)
        x1s, h2s = _merge(yc_s, oa_s, zs, xs, mod_s, *merge_w, n_s, n_s)

        if dense:
            xp = _ffn(h2p, x1p, mod_p, ffn_g, ffn_u, ffn_d, 512, 256, seq)
            xs = _ffn(h2s, x1s, mod_s, ffn_g, ffn_u, ffn_d, n_s, 256, n_s)
        else:
            xp = _moe_routed(h2p, x1p, mod_p, router_pad, moe_g, moe_u, moe_d, final_norm_g,
                             512, 256, 256, seq)
            xs = _moe(h2s, x1s, mod_s, router_pad, moe_g, moe_u, moe_d, final_norm_g, n_s, 256, n_s)

        hgrn_p.append(s_p)
        conv_p.append(c_p)
        hgrn_s.append(s_s)
        conv_s.append(c_s)

    return (xp.reshape(bsz, seq, D_MODEL), xs.reshape(n_s, 1, D_MODEL),
            jnp.stack(hgrn_p), jnp.stack(conv_p), jnp.stack(hgrn_s), jnp.stack(conv_s))
```

```python
import functools

import numpy as np
import jax
import jax.numpy as jnp
from jax import lax
from jax.experimental import pallas as pl
from jax.experimental.pallas import tpu as pltpu

F32 = jnp.float32
BF16 = jnp.bfloat16

D_MODEL = 1024
DEPTH = 2
HEADS = 8
HEAD_DIM = 128
CHUNK = 64
CONV_W = 31
N_EXPERTS = 8
D_FF = 2816
EPS = 1e-6
LOG2_E = 1.4426950408889634
LANES = 128
N_GROUPS_OUT = 7
VMEM_LIMIT = 48 * 1024 * 1024
MOE_VMEM_LIMIT = 56 * 1024 * 1024

LEVEL_WIDTHS = tuple(CHUNK >> (i + 1) for i in range(CHUNK.bit_length() - 1))


def _sigmoid(x):
    return 1.0 / (1.0 + jnp.exp(-x))


def _silu(x):
    return x * _sigmoid(x)


def _cparams(sem):
    return pltpu.CompilerParams(dimension_semantics=sem, vmem_limit_bytes=VMEM_LIMIT)


def _ada_kernel(c_ref, w_ref, b_ref, o_ref):
    a = _silu(c_ref[...]).astype(BF16)
    o_ref[...] = jnp.dot(a, w_ref[...].astype(BF16), preferred_element_type=F32) + b_ref[...]


def _ada(c_all, ada_w, ada_b):
    n = c_all.shape[0]
    tn = 1024
    return pl.pallas_call(
        _ada_kernel,
        grid=(DEPTH, 6 * D_MODEL // tn),
        in_specs=[
            pl.BlockSpec((n, D_MODEL), lambda l, j: (0, 0)),
            pl.BlockSpec((None, D_MODEL, tn), lambda l, j: (l, 0, j)),
            pl.BlockSpec((None, 1, tn), lambda l, j: (l, 0, j)),
        ],
        out_specs=pl.BlockSpec((None, n, tn), lambda l, j: (l, 0, j)),
        out_shape=jax.ShapeDtypeStruct((DEPTH, n, 6 * D_MODEL), F32),
        compiler_params=_cparams(("parallel", "parallel")),
        name="ada",
    )(c_all, ada_w, ada_b.reshape(DEPTH, 1, 6 * D_MODEL))


def _inproj_kernel(x_ref, sh_ref, sc_ref, g_ref, lbl_ref, w_ref, z_ref, lf_ref, *, layer):
    x = x_ref[...]
    y = x * lax.rsqrt(jnp.mean(x * x, axis=-1, keepdims=True) + EPS) * g_ref[...]
    h = (y * (1.0 + sc_ref[...]) + sh_ref[...]).astype(BF16)

    def proj(group):
        return jnp.dot(h, w_ref[:, group * D_MODEL:(group + 1) * D_MODEL],
                       preferred_element_type=F32)

    def out(group):
        return slice(group * D_MODEL, (group + 1) * D_MODEL)

    z_ref[:, out(0)] = _silu(proj(0))
    lbl = lbl_ref[...]
    e = jnp.exp(lbl - jnp.max(lbl, axis=0, keepdims=True))
    p = e / jnp.sum(e, axis=0, keepdims=True)
    cum = p[0:1]
    for i in range(1, layer + 1):
        cum = cum + p[i:i + 1]
    lb = cum - p[0:1]
    zf = proj(1)
    z_ref[:, out(1)] = (1.0 - lb) * _sigmoid(-zf)
    lf_ref[...] = jnp.log(lb + (1.0 - lb) * _sigmoid(zf))
    z_ref[:, out(2)] = proj(2)
    z_ref[:, out(3)] = _silu(proj(3))
    z_ref[:, out(4)] = proj(4) * _sigmoid(proj(5))
    z_ref[:, out(5)] = _sigmoid(proj(6))
    z_ref[:, out(6)] = _sigmoid(proj(7))


def _inproj(x, mod, norm_g, lb_logits, w_in_b, layer, tm, rows_per_mod):
    m = x.shape[0]
    mod_rows = mod.shape[1]
    n_cols = w_in_b.shape[-1]

    def mod_spec(col):
        return pl.BlockSpec((None, mod_rows, D_MODEL),
                            lambda i: ((i * tm) // rows_per_mod, 0, col))

    return pl.pallas_call(
        functools.partial(_inproj_kernel, layer=layer),
        grid=(m // tm,),
        in_specs=[
            pl.BlockSpec((tm, D_MODEL), lambda i: (i, 0)),
            mod_spec(0), mod_spec(1),
            pl.BlockSpec((1, D_MODEL), lambda i: (0, 0)),
            pl.BlockSpec((DEPTH, D_MODEL), lambda i: (0, 0)),
            pl.BlockSpec((None, D_MODEL, n_cols), lambda i: (layer, 0, 0),
                         pipeline_mode=pl.Buffered(1)),
        ],
        out_specs=[
            pl.BlockSpec((tm, N_GROUPS_OUT * D_MODEL), lambda i: (i, 0)),
            pl.BlockSpec((tm, D_MODEL), lambda i: (i, 0)),
        ],
        out_shape=[
            jax.ShapeDtypeStruct((m, N_GROUPS_OUT * D_MODEL), F32),
            jax.ShapeDtypeStruct((m, D_MODEL), F32),
        ],
        compiler_params=_cparams(("parallel",)),
        name="inproj",
    )(x, mod, mod, norm_g.reshape(1, D_MODEL), lb_logits, w_in_b)


def _hgrn_constants():
    t = np.arange(CHUNK)
    s = t[None, :]
    blocks = [(s <= t[:, None])]
    for w in LEVEL_WIDTHS:
        ref_row = ((t & ~(2 * w - 1)) + w - 1)[:, None]
        is_query = ((t & w) != 0)[:, None]
        blocks.append(np.where(is_query, (s > ref_row) & (s <= t[:, None]),
                               (s > t[:, None]) & (s <= ref_row)))
    cum = np.concatenate(blocks, axis=0).astype(np.float32)
    acat = np.concatenate([cum, cum, cum, np.zeros_like(cum)], axis=1)
    x = t[:, None] ^ t[None, :]
    lvl = np.full((CHUNK, CHUNK), -1, np.int32)
    for li, w in enumerate(LEVEL_WIDTHS):
        lvl[((x // w) == 1) & ((t[:, None] & w) != 0)] = li
    lvl[t[:, None] == t[None, :]] = len(LEVEL_WIDTHS)
    return jnp.asarray(acat, BF16), jnp.asarray(np.tile(lvl, (1, 2)))


def _block_diag(a, b):
    z = jnp.zeros_like(a)
    return jnp.concatenate([jnp.concatenate([a, z], axis=1),
                            jnp.concatenate([z, b], axis=1)], axis=0)


def _hgrn_prompt_kernel(q_ref, k_ref, v_ref, og_ref, lf_ref, gn_ref, acat_ref, lvl_ref,
                        o_ref, s_ref, st_scr, *, n_chunks):
    tb = pl.program_id(1)
    seqs = range(st_scr.shape[0])

    @pl.when(tb == 0)
    def _():
        st_scr[...] = jnp.zeros_like(st_scr)

    lvl = lvl_ref[...]
    trans_b = (((1,), (1,)), ((), ()))
    trans_a = (((0,), (0,)), ((), ()))
    pair_w = 2 * HEAD_DIM
    n_pairs = HEADS // 2
    n_lvl = len(LEVEL_WIDTHS)

    def head(x, h):
        return x[:, h * HEAD_DIM:(h + 1) * HEAD_DIM]

    def pair_rows(x, p):
        return _block_diag(head(x, 2 * p), head(x, 2 * p + 1))

    def lanes(p):
        return slice(p * pair_w, (p + 1) * pair_w)

    def chunk_body(c, carry):
        rows = pl.ds(pl.multiple_of(c * CHUNK, CHUNK), CHUNK)
        q = [q_ref[s, rows, :] for s in seqs]
        k = [k_ref[s, rows, :] for s in seqs]
        v = [v_ref[s, rows, :].astype(BF16) for s in seqs]
        cums = []
        for s in seqs:
            lf = lf_ref[s, rows, :] * LOG2_E
            hi = lf.astype(BF16)
            r1 = lf - hi.astype(F32)
            mid = r1.astype(BF16)
            lo = (r1 - mid.astype(F32)).astype(BF16)
            lf4 = jnp.concatenate([hi, mid, lo, jnp.zeros_like(hi)], axis=0)
            cums.append(jnp.dot(acat_ref[...], lf4, preferred_element_type=F32))
        qb = [q[s].astype(BF16) for s in seqs]
        kb = [k[s].astype(BF16) for s in seqs]
        att = [[jnp.where(lvl == n_lvl,
                          lax.dot_general(qb[s][:, lanes(p)], pair_rows(kb[s], p), trans_b,
                                          preferred_element_type=F32), 0.0)
                for p in range(n_pairs)] for s in seqs]
        for li in range(n_lvl):
            for s in seqs:
                e = jnp.exp2(cums[s][(li + 1) * CHUNK:(li + 2) * CHUNK]).astype(BF16)
                xq = qb[s] * e
                xk = kb[s] * e
                for p in range(n_pairs):
                    g = lax.dot_general(xq[:, lanes(p)], pair_rows(xk, p), trans_b,
                                        preferred_element_type=F32)
                    att[s][p] = jnp.where(lvl == li, g, att[s][p])
        inter = []
        for s in seqs:
            b = cums[s][0:CHUNK]
            qe = (q[s] * jnp.exp2(b)).astype(BF16)
            inter.append([lax.dot_general(
                qe[:, lanes(p)],
                _block_diag(st_scr[s, 2 * p].astype(BF16), st_scr[s, 2 * p + 1].astype(BF16)),
                trans_b, preferred_element_type=F32) for p in range(n_pairs)])
        for s in seqs:
            b = cums[s][0:CHUNK]
            b_last = b[CHUNK - 1:CHUNK]
            kd = (k[s] * jnp.exp2(b_last - b)).astype(BF16)
            decay = jnp.exp2(b_last)
            for h in range(HEADS):
                st_scr[s, h] = head(decay, h) * st_scr[s, h] + lax.dot_general(
                    head(v[s], h), head(kd, h), trans_a, preferred_element_type=F32)
        for s in seqs:
            for p in range(n_pairs):
                o_pair = inter[s][p] + jnp.dot(att[s][p].astype(BF16), pair_rows(v[s], p),
                                               preferred_element_type=F32)
                for h in (2 * p, 2 * p + 1):
                    sl = slice(h * HEAD_DIM, (h + 1) * HEAD_DIM)
                    o = head(o_pair, h % 2)
                    o = o * lax.rsqrt(jnp.mean(o * o, axis=-1, keepdims=True) + EPS)
                    o = o * gn_ref[:, sl] * og_ref[s, rows, sl]
                    o_ref[s, rows, sl] = o.astype(BF16)
        return carry

    lax.fori_loop(0, n_chunks, chunk_body, 0)

    @pl.when(tb == pl.num_programs(1) - 1)
    def _():
        for s in seqs:
            for h in range(HEADS):
                s_ref[s, h] = st_scr[s, h].T


def _hgrn_prompt(z3, lf3, gn, acat, lvl, tb, nb):
    bsz, t, _ = z3.shape

    def zspec(col):
        return pl.BlockSpec((nb, tb, D_MODEL), lambda b, i: (b, i, col))

    return pl.pallas_call(
        functools.partial(_hgrn_prompt_kernel, n_chunks=tb // CHUNK),
        grid=(bsz // nb, t // tb),
        in_specs=[
            zspec(0), zspec(1), zspec(2), zspec(3),
            pl.BlockSpec((nb, tb, D_MODEL), lambda b, i: (b, i, 0)),
            pl.BlockSpec((1, D_MODEL), lambda b, i: (0, 0)),
            pl.BlockSpec(acat.shape, lambda b, i: (0, 0)),
            pl.BlockSpec(lvl.shape, lambda b, i: (0, 0)),
        ],
        out_specs=[
            pl.BlockSpec((nb, tb, D_MODEL), lambda b, i: (b, i, 0)),
            pl.BlockSpec((nb, HEADS, HEAD_DIM, HEAD_DIM), lambda b, i: (b, 0, 0, 0)),
        ],
        out_shape=[
            jax.ShapeDtypeStruct((bsz, t, D_MODEL), BF16),
            jax.ShapeDtypeStruct((bsz, HEADS, HEAD_DIM, HEAD_DIM), F32),
        ],
        scratch_shapes=[pltpu.VMEM((nb, HEADS, HEAD_DIM, HEAD_DIM), F32)],
        compiler_params=_cparams(("parallel", "arbitrary")),
        name="hgrn_prompt",
    )(z3, z3, z3, z3, lf3, gn.reshape(1, D_MODEL), acat, lvl)


def _hgrn_sample_kernel(q_ref, k_ref, v_ref, og_ref, lf_ref, gn_ref, s_ref, o_ref, so_ref, o_scr,
                        *, bb):
    n = bb * HEADS
    qt = q_ref[...].reshape(n, HEAD_DIM).T
    kt = k_ref[...].reshape(n, HEAD_DIM).T
    ft = jnp.exp(lf_ref[...].reshape(n, HEAD_DIM)).T
    for b in range(bb):
        for h in range(HEADS):
            c = b * HEADS + h
            s_new = ft[:, c:c + 1] * s_ref[b, h] + kt[:, c:c + 1] * v_ref[b, h:h + 1, :]
            so_ref[b, h] = s_new
            o_scr[c:c + 1, :] = jnp.sum(qt[:, c:c + 1] * s_new, axis=0, keepdims=True)
    o = o_scr[...]
    o = o * lax.rsqrt(jnp.mean(o * o, axis=-1, keepdims=True) + EPS)
    o = o * gn_ref[...] * og_ref[...].reshape(n, HEAD_DIM)
    o_ref[...] = o.astype(BF16)


def _hgrn_sample(z, lf, gn, states, layer, bb):
    n = z.shape[0]
    z3 = z.reshape(n, N_GROUPS_OUT * HEADS, HEAD_DIM)
    lf3 = lf.reshape(n, HEADS, HEAD_DIM)
    gn_rows = jnp.tile(gn.reshape(HEADS, HEAD_DIM), (bb, 1))
    state_block = (bb, HEADS, HEAD_DIM, HEAD_DIM)

    def zspec(col):
        return pl.BlockSpec((bb, HEADS, HEAD_DIM), lambda i: (i, col, 0))

    o, s_new = pl.pallas_call(
        functools.partial(_hgrn_sample_kernel, bb=bb),
        grid=(n // bb,),
        in_specs=[
            zspec(0), zspec(1), zspec(2), zspec(3),
            pl.BlockSpec((bb, HEADS, HEAD_DIM), lambda i: (i, 0, 0)),
            pl.BlockSpec((bb * HEADS, HEAD_DIM), lambda i: (0, 0)),
            pl.BlockSpec((None,) + state_block, lambda i: (layer, i, 0, 0, 0)),
        ],
        out_specs=[
            pl.BlockSpec((bb * HEADS, HEAD_DIM), lambda i: (i, 0)),
            pl.BlockSpec(state_block, lambda i: (i, 0, 0, 0)),
        ],
        out_shape=[
            jax.ShapeDtypeStruct((n * HEADS, HEAD_DIM), BF16),
            jax.ShapeDtypeStruct(states.shape[1:], F32),
        ],
        scratch_shapes=[pltpu.VMEM((bb * HEADS, HEAD_DIM), F32)],
        compiler_params=_cparams(("parallel",)),
        name="hgrn_sample",
    )(z3, z3, z3, z3, lf3, gn_rows, states)
    return o.reshape(n, D_MODEL), s_new


CONV_HIST = 32
CONV_ROWS = 32


def _conv_prompt_kernel(u_ref, wb_ref, bias_ref, y_ref, buf, shifted, *, tc):
    @pl.when(pl.program_id(1) == 0)
    def _():
        buf[0:CONV_HIST, :] = jnp.zeros((CONV_HIST, D_MODEL), F32)

    buf[CONV_HIST:CONV_HIST + tc, :] = u_ref[...]
    lead = CONV_HIST - (CONV_W - 1)
    n_shift = shifted.shape[1]
    for s in range(1, 8):
        shifted[s - 1] = buf[s:s + n_shift, :]

    def rows_body(r, carry):
        base = pl.multiple_of(r * CONV_ROWS, CONV_ROWS)
        accs = [jnp.broadcast_to(bias_ref[...], (8, D_MODEL)) for _ in range(CONV_ROWS // 8)]
        for j in range(CONV_W):
            wj = wb_ref[j * 8:(j + 1) * 8, :]
            p, s = divmod(lead + j, 8)
            for a in range(CONV_ROWS // 8):
                rows = pl.ds(base + 8 * (a + p), 8)
                tap = buf[rows, :] if s == 0 else shifted[s - 1, rows, :]
                accs[a] = accs[a] + wj * tap
        for a in range(CONV_ROWS // 8):
            y_ref[pl.ds(base + a * 8, 8), :] = accs[a]
        return carry

    lax.fori_loop(0, tc // CONV_ROWS, rows_body, 0)
    buf[0:CONV_HIST, :] = buf[tc:tc + CONV_HIST, :]


def _conv_prompt(z3, conv_w, conv_b, tc):
    bsz, t, _ = z3.shape
    wb = jnp.repeat(conv_w, 8, axis=0)
    return pl.pallas_call(
        functools.partial(_conv_prompt_kernel, tc=tc),
        grid=(bsz, t // tc),
        in_specs=[
            pl.BlockSpec((None, tc, D_MODEL), lambda b, i: (b, i, 4)),
            pl.BlockSpec((CONV_W * 8, D_MODEL), lambda b, i: (0, 0)),
            pl.BlockSpec((1, D_MODEL), lambda b, i: (0, 0)),
        ],
        out_specs=pl.BlockSpec((None, tc, D_MODEL), lambda b, i: (b, i, 0)),
        out_shape=jax.ShapeDtypeStruct((bsz, t, D_MODEL), F32),
        scratch_shapes=[pltpu.VMEM((CONV_HIST + tc, D_MODEL), F32),
                        pltpu.VMEM((7, CONV_HIST + tc - 8, D_MODEL), F32)],
        compiler_params=_cparams(("parallel", "arbitrary")),
        name="conv_prompt",
    )(z3, wb, conv_b.reshape(1, D_MODEL))


def _conv_sample_kernel(u_ref, w_ref, bias_ref, s_ref, y_ref, so_ref, *, bb):
    hist = CONV_W - 1
    for b in range(bb):
        u = u_ref[b:b + 1, :]
        y = jnp.sum(s_ref[b] * w_ref[0:hist, :], axis=0, keepdims=True)
        y_ref[b:b + 1, :] = y + w_ref[hist:CONV_W, :] * u + bias_ref[...]
        so_ref[b, 0:hist - 1, :] = s_ref[b, 1:hist, :]
        so_ref[b, hist - 1:hist, :] = u


def _conv_sample(z, states, layer, conv_w, conv_b, bb):
    n = z.shape[0]
    state_block = (bb, CONV_W - 1, D_MODEL)
    return pl.pallas_call(
        functools.partial(_conv_sample_kernel, bb=bb),
        grid=(n // bb,),
        in_specs=[
            pl.BlockSpec((bb, D_MODEL), lambda i: (i, 4)),
            pl.BlockSpec((CONV_W, D_MODEL), lambda i: (0, 0)),
            pl.BlockSpec((1, D_MODEL), lambda i: (0, 0)),
            pl.BlockSpec((None,) + state_block, lambda i: (layer, i, 0, 0)),
        ],
        out_specs=[pl.BlockSpec((bb, D_MODEL), lambda i: (i, 0)),
                   pl.BlockSpec(state_block, lambda i: (i, 0, 0))],
        out_shape=[jax.ShapeDtypeStruct((n, D_MODEL), F32),
                   jax.ShapeDtypeStruct(states.shape[1:], F32)],
        compiler_params=_cparams(("parallel",)),
        name="conv_sample",
    )(z, conv_w, conv_b.reshape(1, D_MODEL), states)


def _merge_kernel(yc_ref, oa_ref, ga_ref, gb_ref, x_ref, g1_ref, sh2_ref, sc2_ref,
                  lng_ref, lnb_ref, n2g_ref, wco_ref, whg_ref, wo_ref, x1_ref, h2_ref):
    yc = yc_ref[...]
    xc = yc - jnp.mean(yc, axis=-1, keepdims=True)
    vln = xc * lax.rsqrt(jnp.mean(xc * xc, axis=-1, keepdims=True) + EPS)
    va = _silu(vln * lng_ref[...] + lnb_ref[...]).astype(BF16)
    y_b = jnp.dot(va, wco_ref[...], preferred_element_type=F32)
    y_a = jnp.dot(oa_ref[...], whg_ref[...], preferred_element_type=F32)
    mixed = (ga_ref[...] * y_a + gb_ref[...] * y_b).astype(BF16)
    x1 = x_ref[...] + g1_ref[...] * jnp.dot(mixed, wo_ref[...], preferred_element_type=F32)
    x1_ref[...] = x1
    y = x1 * lax.rsqrt(jnp.mean(x1 * x1, axis=-1, keepdims=True) + EPS) * n2g_ref[...]
    h2_ref[...] = (y * (1.0 + sc2_ref[...]) + sh2_ref[...]).astype(h2_ref.dtype)


def _merge(yc, oa, z, x, mod, ln_g, ln_b, n2_g, wco, whg, wo, layer, tm, rows_per_mod,
           h2_dtype=BF16):
    m = x.shape[0]
    mod_rows = mod.shape[1]

    def mod_spec(col):
        return pl.BlockSpec((None, mod_rows, D_MODEL),
                            lambda i: ((i * tm) // rows_per_mod, 0, col))

    def row_spec(col=0):
        return pl.BlockSpec((tm, D_MODEL), lambda i: (i, col))

    vec_spec = pl.BlockSpec((1, D_MODEL), lambda i: (0, 0))
    w_spec = pl.BlockSpec((None, D_MODEL, D_MODEL), lambda i: (layer, 0, 0))
    return pl.pallas_call(
        _merge_kernel,
        grid=(m // tm,),
        in_specs=[row_spec(), row_spec(), row_spec(5), row_spec(6), row_spec(),
                  mod_spec(2), mod_spec(3), mod_spec(4),
                  vec_spec, vec_spec, vec_spec, w_spec, w_spec, w_spec],
        out_specs=[row_spec(), row_spec()],
        out_shape=[jax.ShapeDtypeStruct((m, D_MODEL), F32),
                   jax.ShapeDtypeStruct((m, D_MODEL), h2_dtype)],
        compiler_params=_cparams(("parallel",)),
        name="merge",
    )(yc, oa, z, z, x, mod, mod, mod,
      ln_g.reshape(1, D_MODEL), ln_b.reshape(1, D_MODEL), n2_g.reshape(1, D_MODEL),
      wco, whg, wo)


def _swiglu_rows(x, wg_ref, wu_ref, wd_ref, tf, between=None):
    acc = None
    for j in range(D_FF // tf):
        if between is not None:
            between(j)
        cols = slice(j * tf, (j + 1) * tf)
        a = _silu(jnp.dot(x, wg_ref[:, cols], preferred_element_type=F32))
        a = a * jnp.dot(x, wu_ref[:, cols], preferred_element_type=F32)
        d = jnp.dot(a.astype(BF16), wd_ref[cols, :], preferred_element_type=F32)
        acc = d if acc is None else acc + d
    return acc


def _ffn_kernel(h_ref, x1_ref, g2_ref, wg_ref, wu_ref, wd_ref, o_ref, *, tf):
    o_ref[...] = x1_ref[...] + g2_ref[...] * _swiglu_rows(h_ref[...], wg_ref, wu_ref, wd_ref, tf)


def _ffn(h2, x1, mod, wg, wu, wd, tm, tf, rows_per_mod):
    m = x1.shape[0]
    mod_rows = mod.shape[1]
    w_up_spec = pl.BlockSpec((None, D_MODEL, D_FF), lambda i: (0, 0, 0), pipeline_mode=pl.Buffered(1))
    return pl.pallas_call(
        functools.partial(_ffn_kernel, tf=tf),
        grid=(m // tm,),
        in_specs=[
            pl.BlockSpec((tm, D_MODEL), lambda i: (i, 0)),
            pl.BlockSpec((tm, D_MODEL), lambda i: (i, 0)),
            pl.BlockSpec((None, mod_rows, D_MODEL), lambda i: ((i * tm) // rows_per_mod, 0, 5)),
            w_up_spec, w_up_spec,
            pl.BlockSpec((None, D_FF, D_MODEL), lambda i: (0, 0, 0), pipeline_mode=pl.Buffered(1)),
        ],
        out_specs=pl.BlockSpec((tm, D_MODEL), lambda i: (i, 0)),
        out_shape=jax.ShapeDtypeStruct((m, D_MODEL), F32),
        compiler_params=_cparams(("parallel",)),
        name="ffn",
    )(h2, x1, mod, wg, wu, wd)


def _top2(logits, lane):
    neg = jnp.float32(-jnp.inf)
    logits = jnp.where(lane < N_EXPERTS, logits, neg)
    v1 = jnp.max(logits, axis=-1, keepdims=True)
    i1 = jnp.min(jnp.where(logits == v1, lane, LANES), axis=-1, keepdims=True)
    rest = jnp.where(lane == i1, neg, logits)
    v2 = jnp.max(rest, axis=-1, keepdims=True)
    i2 = jnp.min(jnp.where(rest == v2, lane, LANES), axis=-1, keepdims=True)
    e2 = jnp.exp(v2 - v1)
    w1 = 1.0 / (1.0 + e2)
    return i1, i2, w1, e2 * w1


def _route_kernel(h_ref, rw_ref, r_ref):
    lane = lax.broadcasted_iota(jnp.int32, r_ref.shape, 1)
    logits = jnp.dot(h_ref[...].astype(BF16), rw_ref[...], preferred_element_type=F32)
    i1, i2, w1, w2 = _top2(logits, lane)
    r_ref[...] = (jnp.where(lane == 0, i1.astype(F32), 0.0) + jnp.where(lane == 1, i2.astype(F32), 0.0)
                  + jnp.where(lane == 2, w1, 0.0) + jnp.where(lane == 3, w2, 0.0))


def _route(h2, router_pad, tm):
    m = h2.shape[0]
    return pl.pallas_call(
        _route_kernel,
        grid=(m // tm,),
        in_specs=[pl.BlockSpec((tm, D_MODEL), lambda i: (i, 0)),
                  pl.BlockSpec((D_MODEL, LANES), lambda i: (0, 0))],
        out_specs=pl.BlockSpec((tm, LANES), lambda i: (i, 0)),
        out_shape=jax.ShapeDtypeStruct((m, LANES), F32),
        compiler_params=_cparams(("parallel",)),
        name="route",
    )(h2, router_pad)


def _start_row_gather(idx_ref, n_rows, src_hbm, dst, sem):
    def body(r, carry):
        pltpu.make_async_copy(src_hbm.at[pl.ds(idx_ref[r], 1), :], dst.at[pl.ds(r, 1), :], sem).start()
        return carry
    lax.fori_loop(0, n_rows, body, 0, unroll=8)


def _moe_sparse_kernel(te_ref, na_ref, idx_ref, idx_next_ref, h_hbm, wg_ref, wu_ref, wd_ref,
                       y_ref, buf, sem, *, tf):
    i = pl.program_id(0)
    n_active = na_ref[0]
    slot = lax.rem(i, 2)
    tm = buf.shape[1]
    n_chunks = D_FF // tf
    per_chunk = -(-tm // n_chunks)

    @pl.when(i == 0)
    def _():
        _start_row_gather(idx_ref.at[0], tm, h_hbm, buf.at[0], sem.at[0])

    pltpu.make_async_copy(buf.at[slot], buf.at[slot], sem.at[slot]).wait()

    @pl.when(i < n_active)
    def _():
        def fetch_next(j):
            for r in range(j * per_chunk, min(tm, (j + 1) * per_chunk)):
                pltpu.make_async_copy(h_hbm.at[pl.ds(idx_next_ref[0, r], 1), :],
                                      buf.at[1 - slot, pl.ds(r, 1), :], sem.at[1 - slot]).start()

        y_ref[...] = _swiglu_rows(buf[slot].astype(BF16), wg_ref, wu_ref, wd_ref, tf, fetch_next)

    @pl.when(i >= n_active)
    def _():
        @pl.when(i + 1 < pl.num_programs(0))
        def _():
            _start_row_gather(idx_next_ref.at[0], tm, h_hbm, buf.at[1 - slot], sem.at[1 - slot])

        y_ref[...] = jnp.zeros_like(y_ref)


def _moe_sparse(h2, src_tok, tile_expert, n_active, wg, wu, wd, tm, tf):
    n_tiles = src_tok.shape[0]
    idx_spec = lambda f: pl.BlockSpec((None, 1, tm), f, memory_space=pltpu.SMEM)
    w_up_spec = pl.BlockSpec((None, None, D_MODEL, D_FF), lambda i, te, na: (0, te[i], 0, 0))
    grid_spec = pltpu.PrefetchScalarGridSpec(
        num_scalar_prefetch=2,
        grid=(n_tiles,),
        in_specs=[
            idx_spec(lambda i, te, na: (i, 0, 0)),
            idx_spec(lambda i, te, na: (jnp.minimum(i + 1, n_tiles - 1), 0, 0)),
            pl.BlockSpec(memory_space=pl.ANY),
            w_up_spec, w_up_spec,
            pl.BlockSpec((None, None, D_FF, D_MODEL), lambda i, te, na: (0, te[i], 0, 0)),
        ],
        out_specs=pl.BlockSpec((tm, D_MODEL), lambda i, te, na: (i, 0)),
        scratch_shapes=[pltpu.VMEM((2, tm, D_MODEL), F32), pltpu.SemaphoreType.DMA((2,))],
    )
    return pl.pallas_call(
        functools.partial(_moe_sparse_kernel, tf=tf),
        grid_spec=grid_spec,
        out_shape=jax.ShapeDtypeStruct((n_tiles * tm, D_MODEL), F32),
        compiler_params=pltpu.CompilerParams(dimension_semantics=("arbitrary",),
                                             vmem_limit_bytes=MOE_VMEM_LIMIT),
        name="moe_sparse",
    )(tile_expert, n_active, src_tok, src_tok, h2, wg, wu, wd)


def _moe_combine_kernel(pos_ref, pos_next_ref, y_hbm, r_ref, x1_ref, g2_ref, fg_ref, o_ref,
                        buf, sem):
    i = pl.program_id(0)
    slot = lax.rem(i, 2)
    n_rows = buf.shape[1]

    @pl.when(i == 0)
    def _():
        _start_row_gather(pos_ref.at[0], n_rows, y_hbm, buf.at[0], sem.at[0])

    @pl.when(i + 1 < pl.num_programs(0))
    def _():
        _start_row_gather(pos_next_ref.at[0], n_rows, y_hbm, buf.at[1 - slot], sem.at[1 - slot])

    pltpu.make_async_copy(buf.at[slot], buf.at[slot], sem.at[slot]).wait()
    tc = n_rows // 2
    lane = lax.broadcasted_iota(jnp.int32, r_ref.shape, 1)
    r = r_ref[...]
    w1 = jnp.sum(jnp.where(lane == 2, r, 0.0), axis=-1, keepdims=True)
    w2 = jnp.sum(jnp.where(lane == 3, r, 0.0), axis=-1, keepdims=True)
    f = w1 * buf[slot, 0:tc, :] + w2 * buf[slot, tc:n_rows, :]
    x2 = x1_ref[...] + g2_ref[...] * f
    o_ref[...] = x2 * lax.rsqrt(jnp.mean(x2 * x2, axis=-1, keepdims=True) + EPS) * fg_ref[...]


def _moe_combine(y_sorted, pos, route, x1, mod, final_g, tc, rows_per_mod):
    m = x1.shape[0]
    n_blocks = m // tc
    mod_rows = mod.shape[1]
    pos_spec = lambda f: pl.BlockSpec((None, 1, 2 * tc), f, memory_space=pltpu.SMEM)
    return pl.pallas_call(
        _moe_combine_kernel,
        grid=(n_blocks,),
        in_specs=[
            pos_spec(lambda i: (i, 0, 0)),
            pos_spec(lambda i: (jnp.minimum(i + 1, n_blocks - 1), 0, 0)),
            pl.BlockSpec(memory_space=pl.ANY),
            pl.BlockSpec((tc, LANES), lambda i: (i, 0)),
            pl.BlockSpec((tc, D_MODEL), lambda i: (i, 0)),
            pl.BlockSpec((None, mod_rows, D_MODEL), lambda i: ((i * tc) // rows_per_mod, 0, 5)),
            pl.BlockSpec((1, D_MODEL), lambda i: (0, 0)),
        ],
        out_specs=pl.BlockSpec((tc, D_MODEL), lambda i: (i, 0)),
        out_shape=jax.ShapeDtypeStruct((m, D_MODEL), F32),
        scratch_shapes=[pltpu.VMEM((2, 2 * tc, D_MODEL), F32), pltpu.SemaphoreType.DMA((2,))],
        compiler_params=_cparams(("arbitrary",)),
        name="moe_combine",
    )(pos, pos, y_sorted, route, x1, mod, final_g.reshape(1, D_MODEL))


def _moe_routed(h2, x1, mod, router_pad, wg, wu, wd, final_g, tm, tf, tc, rows_per_mod):
    n = h2.shape[0]
    route = _route(h2, router_pad, 512)
    expert = route[:, :2].astype(jnp.int32).reshape(2 * n)
    onehot = (expert[:, None] == jnp.arange(N_EXPERTS, dtype=jnp.int32)[None, :]).astype(jnp.int32)
    running = jnp.cumsum(onehot, axis=0)
    rank = jnp.sum((running - onehot) * onehot, axis=1)
    padded = ((running[-1] + tm - 1) // tm) * tm
    group_end = jnp.cumsum(padded)
    pos = (group_end - padded)[expert] + rank
    n_tiles = (2 * n) // tm + N_EXPERTS
    src_tok = jnp.zeros((n_tiles * tm,), jnp.int32).at[pos].set(
        jnp.arange(2 * n, dtype=jnp.int32) // 2, unique_indices=True)
    tile_start = jnp.arange(n_tiles, dtype=jnp.int32) * tm
    tile_expert = jnp.minimum(jnp.sum(tile_start[:, None] >= group_end[None, :], axis=1),
                              N_EXPERTS - 1).astype(jnp.int32)
    n_active = (group_end[-1:] // tm).astype(jnp.int32)

    y_sorted = _moe_sparse(h2, src_tok.reshape(n_tiles, 1, tm), tile_expert, n_active,
                           wg, wu, wd, tm, tf)
    pos_blocks = pos.reshape(n // tc, tc, 2).transpose(0, 2, 1).reshape(n // tc, 1, 2 * tc)
    return _moe_combine(y_sorted, pos_blocks, route, x1, mod, final_g, tc, rows_per_mod)


def _moe_kernel(h_ref, x1_ref, g2_ref, rw_ref, wg_ref, wu_ref, wd_ref, fg_ref, o_ref,
                acc, comb):
    e = pl.program_id(1)
    j = pl.program_id(2)
    lane = lax.broadcasted_iota(jnp.int32, comb.shape, 1)

    @pl.when((e == 0) & (j == 0))
    def _():
        acc[...] = jnp.zeros_like(acc)
        i1, i2, w1, w2 = _top2(jnp.dot(h_ref[...], rw_ref[...], preferred_element_type=F32), lane)
        comb[...] = jnp.where(lane == i1, w1, 0.0) + jnp.where(lane == i2, w2, 0.0)

    ce = jnp.sum(jnp.where(lane == e, comb[...], 0.0), axis=-1, keepdims=True)
    h = h_ref[...]
    a = _silu(jnp.dot(h, wg_ref[...], preferred_element_type=F32))
    a = a * jnp.dot(h, wu_ref[...], preferred_element_type=F32) * ce
    acc[...] += jnp.dot(a.astype(BF16), wd_ref[...], preferred_element_type=F32)

    @pl.when((e == pl.num_programs(1) - 1) & (j == pl.num_programs(2) - 1))
    def _():
        x2 = x1_ref[...] + g2_ref[...] * acc[...]
        o_ref[...] = x2 * lax.rsqrt(jnp.mean(x2 * x2, axis=-1, keepdims=True) + EPS) * fg_ref[...]


def _moe(h2, x1, mod, router_pad, wg, wu, wd, final_g, tm, tf, rows_per_mod):
    m = x1.shape[0]
    mod_rows = mod.shape[1]
    return pl.pallas_call(
        _moe_kernel,
        grid=(m // tm, N_EXPERTS, D_FF // tf),
        in_specs=[
            pl.BlockSpec((tm, D_MODEL), lambda i, e, j: (i, 0)),
            pl.BlockSpec((tm, D_MODEL), lambda i, e, j: (i, 0)),
            pl.BlockSpec((None, mod_rows, D_MODEL),
                         lambda i, e, j: ((i * tm) // rows_per_mod, 0, 5)),
            pl.BlockSpec((D_MODEL, LANES), lambda i, e, j: (0, 0)),
            pl.BlockSpec((None, None, D_MODEL, tf), lambda i, e, j: (0, e, 0, j)),
            pl.BlockSpec((None, None, D_MODEL, tf), lambda i, e, j: (0, e, 0, j)),
            pl.BlockSpec((None, None, tf, D_MODEL), lambda i, e, j: (0, e, j, 0)),
            pl.BlockSpec((1, D_MODEL), lambda i, e, j: (0, 0)),
        ],
        out_specs=pl.BlockSpec((tm, D_MODEL), lambda i, e, j: (i, 0)),
        out_shape=jax.ShapeDtypeStruct((m, D_MODEL), F32),
        scratch_shapes=[pltpu.VMEM((tm, D_MODEL), F32), pltpu.VMEM((tm, LANES), F32)],
        compiler_params=_cparams(("parallel", "arbitrary", "arbitrary")),
        name="moe",
    )(h2, x1, mod, router_pad, wg, wu, wd, final_g.reshape(1, D_MODEL))


def kernel(x_prompt, x_sample, state_hgrn, state_conv, c_prompt, c_sample, ada_w, ada_b, norm1_g, norm2_g, w_in, lb_logits, hg_norm_g, w_hg_out, conv_w, conv_b, conv_ln_g, conv_ln_b, w_conv_out, w_o, ffn_w_gate, ffn_w_up, ffn_w_down, router_w, moe_w_gate, moe_w_up, moe_w_down, final_norm_g):
    bsz, seq, _ = x_prompt.shape
    n_s = x_sample.shape[0]
    hist = CONV_W - 1

    w_in_b = w_in.astype(BF16)
    whg_b = w_hg_out.astype(BF16)
    wco_b = w_conv_out.astype(BF16)
    wo_b = w_o.astype(BF16)
    ffn_g, ffn_u, ffn_d = (w.astype(BF16) for w in (ffn_w_gate, ffn_w_up, ffn_w_down))
    moe_g, moe_u, moe_d = (w.astype(BF16) for w in (moe_w_gate, moe_w_up, moe_w_down))
    router_pad = jnp.pad(router_w[0], ((0, 0), (0, LANES - N_EXPERTS))).astype(BF16)
    acat, lvl = _hgrn_constants()

    mod_all = _ada(jnp.concatenate([c_prompt, c_sample], axis=0), ada_w, ada_b)

    xp = x_prompt.reshape(bsz * seq, D_MODEL)
    xs = x_sample.reshape(n_s, D_MODEL)
    tm_p = 256
    hgrn_p, conv_p, hgrn_s, conv_s = [], [], [], []
    for l in range(DEPTH):
        mod_p = mod_all[l, :bsz].reshape(bsz, 1, 6 * D_MODEL)
        mod_s = mod_all[l, bsz:].reshape(1, n_s, 6 * D_MODEL)

        zp, lfp = _inproj(xp, mod_p, norm1_g[l], lb_logits, w_in_b, l, tm_p, seq)
        zs, lfs = _inproj(xs, mod_s, norm1_g[l], lb_logits, w_in_b, l, n_s, n_s)

        zp3 = zp.reshape(bsz, seq, N_GROUPS_OUT * D_MODEL)
        oa_p, s_p = _hgrn_prompt(zp3, lfp.reshape(bsz, seq, D_MODEL), hg_norm_g[l], acat, lvl, 256, 2)
        oa_s, s_s = _hgrn_sample(zs, lfs, hg_norm_g[l], state_hgrn, l, 8)

        yc_p = _conv_prompt(zp3, conv_w[l], conv_b[l], 256)
        yc_s, c_s = _conv_sample(zs, state_conv, l, conv_w[l], conv_b[l], 8)
        c_p = zp3[:, seq - hist:, 4 * D_MODEL:5 * D_MODEL]

        merge_w = (conv_ln_g[l], conv_ln_b[l], norm2_g[l], wco_b, whg_b, wo_b, l)
        dense = l % 2 == 0
        x1p, h2p = _merge(yc_p.reshape(bsz * seq, D_MODEL), oa_p.reshape(bsz * seq, D_MODEL),
                          zp, xp, mod_p, *merge_w, 256, seq, BF16 if dense else F32)
        x1s, h2s = _merge(yc_s, oa_s, zs, xs, mod_s, *merge_w, n_s, n_s)

        if dense:
            xp = _ffn(h2p, x1p, mod_p, ffn_g, ffn_u, ffn_d, 512, 256, seq)
            xs = _ffn(h2s, x1s, mod_s, ffn_g, ffn_u, ffn_d, n_s, 256, n_s)
        else:
            xp = _moe_routed(h2p, x1p, mod_p, router_pad, moe_g, moe_u, moe_d, final_norm_g,
                             512, 256, 256, seq)
            xs = _moe(h2s, x1s, mod_s, router_pad, moe_g, moe_u, moe_d, final_norm_g, n_s, 256, n_s)

        hgrn_p.append(s_p)
        conv_p.append(c_p)
        hgrn_s.append(s_s)
        conv_s.append(c_s)

    return (xp.reshape(bsz, seq, D_MODEL), xs.reshape(n_s, 1, D_MODEL),
            jnp.stack(hgrn_p), jnp.stack(conv_p), jnp.stack(hgrn_s), jnp.stack(conv_s))
```

```python
import functools

import numpy as np
import jax
import jax.numpy as jnp
from jax import lax
from jax.experimental import pallas as pl
from jax.experimental.pallas import tpu as pltpu

F32 = jnp.float32
BF16 = jnp.bfloat16

D_MODEL = 1024
DEPTH = 2
HEADS = 8
HEAD_DIM = 128
CHUNK = 64
CONV_W = 31
N_EXPERTS = 8
D_FF = 2816
EPS = 1e-6
LOG2_E = 1.4426950408889634
LANES = 128
N_GROUPS_OUT = 6
VMEM_LIMIT = 48 * 1024 * 1024
MOE_VMEM_LIMIT = 56 * 1024 * 1024

LEVEL_WIDTHS = tuple(CHUNK >> (i + 1) for i in range(CHUNK.bit_length() - 1))


def _sigmoid(x):
    return 1.0 / (1.0 + jnp.exp(-x))


def _silu(x):
    return x * _sigmoid(x)


def _cparams(sem):
    return pltpu.CompilerParams(dimension_semantics=sem, vmem_limit_bytes=VMEM_LIMIT)


def _ada_kernel(c_ref, w_ref, b_ref, o_ref):
    a = _silu(c_ref[...]).astype(BF16)
    o_ref[...] = jnp.dot(a, w_ref[...].astype(BF16), preferred_element_type=F32) + b_ref[...]


def _ada(c_all, ada_w, ada_b):
    n = c_all.shape[0]
    tn = 1024
    return pl.pallas_call(
        _ada_kernel,
        grid=(DEPTH, 6 * D_MODEL // tn),
        in_specs=[
            pl.BlockSpec((n, D_MODEL), lambda l, j: (0, 0)),
            pl.BlockSpec((None, D_MODEL, tn), lambda l, j: (l, 0, j)),
            pl.BlockSpec((None, 1, tn), lambda l, j: (l, 0, j)),
        ],
        out_specs=pl.BlockSpec((None, n, tn), lambda l, j: (l, 0, j)),
        out_shape=jax.ShapeDtypeStruct((DEPTH, n, 6 * D_MODEL), F32),
        compiler_params=_cparams(("parallel", "parallel")),
        name="ada",
    )(c_all, ada_w, ada_b.reshape(DEPTH, 1, 6 * D_MODEL))


def _inproj_kernel(x_ref, sh_ref, sc_ref, g_ref, lbl_ref, w_ref, z_ref, u_ref, lf_ref, *, layer):
    x = x_ref[...]
    y = x * lax.rsqrt(jnp.mean(x * x, axis=-1, keepdims=True) + EPS) * g_ref[...]
    h = (y * (1.0 + sc_ref[...]) + sh_ref[...]).astype(BF16)

    def proj(group):
        return jnp.dot(h, w_ref[:, group * D_MODEL:(group + 1) * D_MODEL],
                       preferred_element_type=F32)

    def put(group, value):
        z_ref[:, group * D_MODEL:(group + 1) * D_MODEL] = value.astype(z_ref.dtype)

    put(0, _silu(proj(0)))
    lbl = lbl_ref[...]
    e = jnp.exp(lbl - jnp.max(lbl, axis=0, keepdims=True))
    p = e / jnp.sum(e, axis=0, keepdims=True)
    cum = p[0:1]
    for i in range(1, layer + 1):
        cum = cum + p[i:i + 1]
    lb = cum - p[0:1]
    zf = proj(1)
    put(1, (1.0 - lb) * _sigmoid(-zf))
    lf_ref[...] = jnp.log(lb + (1.0 - lb) * _sigmoid(zf))
    put(2, proj(2))
    put(3, _silu(proj(3)))
    u_ref[...] = proj(4) * _sigmoid(proj(5))
    put(4, _sigmoid(proj(6)))
    put(5, _sigmoid(proj(7)))


def _inproj(x, mod, norm_g, lb_logits, w_in_b, layer, tm, rows_per_mod, z_dtype):
    m = x.shape[0]
    mod_rows = mod.shape[1]
    n_cols = w_in_b.shape[-1]

    def mod_spec(col):
        return pl.BlockSpec((None, mod_rows, D_MODEL),
                            lambda i: ((i * tm) // rows_per_mod, 0, col))

    return pl.pallas_call(
        functools.partial(_inproj_kernel, layer=layer),
        grid=(m // tm,),
        in_specs=[
            pl.BlockSpec((tm, D_MODEL), lambda i: (i, 0)),
            mod_spec(0), mod_spec(1),
            pl.BlockSpec((1, D_MODEL), lambda i: (0, 0)),
            pl.BlockSpec((DEPTH, D_MODEL), lambda i: (0, 0)),
            pl.BlockSpec((None, D_MODEL, n_cols), lambda i: (layer, 0, 0),
                         pipeline_mode=pl.Buffered(1)),
        ],
        out_specs=[
            pl.BlockSpec((tm, N_GROUPS_OUT * D_MODEL), lambda i: (i, 0)),
            pl.BlockSpec((tm, D_MODEL), lambda i: (i, 0)),
            pl.BlockSpec((tm, D_MODEL), lambda i: (i, 0)),
        ],
        out_shape=[
            jax.ShapeDtypeStruct((m, N_GROUPS_OUT * D_MODEL), z_dtype),
            jax.ShapeDtypeStruct((m, D_MODEL), F32),
            jax.ShapeDtypeStruct((m, D_MODEL), F32),
        ],
        compiler_params=_cparams(("parallel",)),
        name="inproj",
    )(x, mod, mod, norm_g.reshape(1, D_MODEL), lb_logits, w_in_b)


def _hgrn_constants():
    t = np.arange(CHUNK)
    s = t[None, :]
    blocks = [(s <= t[:, None])]
    for w in LEVEL_WIDTHS:
        ref_row = ((t & ~(2 * w - 1)) + w - 1)[:, None]
        is_query = ((t & w) != 0)[:, None]
        blocks.append(np.where(is_query, (s > ref_row) & (s <= t[:, None]),
                               (s > t[:, None]) & (s <= ref_row)))
    cum = np.concatenate(blocks, axis=0).astype(np.float32)
    acat = np.concatenate([cum, cum, cum, np.zeros_like(cum)], axis=1)
    x = t[:, None] ^ t[None, :]
    lvl = np.full((CHUNK, CHUNK), -1, np.int32)
    for li, w in enumerate(LEVEL_WIDTHS):
        lvl[((x // w) == 1) & ((t[:, None] & w) != 0)] = li
    lvl[t[:, None] == t[None, :]] = len(LEVEL_WIDTHS)
    return jnp.asarray(acat, BF16), jnp.asarray(np.tile(lvl, (1, 2)))


def _block_diag(a, b):
    z = jnp.zeros_like(a)
    return jnp.concatenate([jnp.concatenate([a, z], axis=1),
                            jnp.concatenate([z, b], axis=1)], axis=0)


def _hgrn_prompt_kernel(q_ref, k_ref, v_ref, og_ref, lf_ref, gn_ref, acat_ref, lvl_ref,
                        o_ref, s_ref, st_scr, *, n_chunks):
    tb = pl.program_id(1)
    seqs = range(st_scr.shape[0])

    @pl.when(tb == 0)
    def _():
        st_scr[...] = jnp.zeros_like(st_scr)

    lvl = lvl_ref[...]
    trans_b = (((1,), (1,)), ((), ()))
    trans_a = (((0,), (0,)), ((), ()))
    pair_w = 2 * HEAD_DIM
    n_pairs = HEADS // 2
    n_lvl = len(LEVEL_WIDTHS)

    def head(x, h):
        return x[:, h * HEAD_DIM:(h + 1) * HEAD_DIM]

    def pair_rows(x, p):
        return _block_diag(head(x, 2 * p), head(x, 2 * p + 1))

    def lanes(p):
        return slice(p * pair_w, (p + 1) * pair_w)

    def chunk_body(c, carry):
        rows = pl.ds(pl.multiple_of(c * CHUNK, CHUNK), CHUNK)
        qb = [q_ref[s, rows, :] for s in seqs]
        kb = [k_ref[s, rows, :] for s in seqs]
        v = [v_ref[s, rows, :] for s in seqs]
        cums = []
        for s in seqs:
            lf = lf_ref[s, rows, :] * LOG2_E
            hi = lf.astype(BF16)
            r1 = lf - hi.astype(F32)
            mid = r1.astype(BF16)
            lo = (r1 - mid.astype(F32)).astype(BF16)
            lf4 = jnp.concatenate([hi, mid, lo, jnp.zeros_like(hi)], axis=0)
            cums.append(jnp.dot(acat_ref[...], lf4, preferred_element_type=F32))
        att = [[jnp.where(lvl == n_lvl,
                          lax.dot_general(qb[s][:, lanes(p)], pair_rows(kb[s], p), trans_b,
                                          preferred_element_type=F32), 0.0)
                for p in range(n_pairs)] for s in seqs]
        for li in range(n_lvl):
            for s in seqs:
                e = jnp.exp2(cums[s][(li + 1) * CHUNK:(li + 2) * CHUNK]).astype(BF16)
                xq = qb[s] * e
                xk = kb[s] * e
                for p in range(n_pairs):
                    g = lax.dot_general(xq[:, lanes(p)], pair_rows(xk, p), trans_b,
                                        preferred_element_type=F32)
                    att[s][p] = jnp.where(lvl == li, g, att[s][p])
        inter = []
        for s in seqs:
            b = cums[s][0:CHUNK]
            qe = qb[s] * jnp.exp2(b).astype(BF16)
            inter.append([lax.dot_general(
                qe[:, lanes(p)],
                _block_diag(st_scr[s, 2 * p].astype(BF16), st_scr[s, 2 * p + 1].astype(BF16)),
                trans_b, preferred_element_type=F32) for p in range(n_pairs)])
        for s in seqs:
            b = cums[s][0:CHUNK]
            b_last = b[CHUNK - 1:CHUNK]
            kd = kb[s] * jnp.exp2(b_last - b).astype(BF16)
            decay = jnp.exp2(b_last)
            for h in range(HEADS):
                st_scr[s, h] = head(decay, h) * st_scr[s, h] + lax.dot_general(
                    head(v[s], h), head(kd, h), trans_a, preferred_element_type=F32)
        for s in seqs:
            for p in range(n_pairs):
                o_pair = inter[s][p] + jnp.dot(att[s][p].astype(BF16), pair_rows(v[s], p),
                                               preferred_element_type=F32)
                for h in (2 * p, 2 * p + 1):
                    sl = slice(h * HEAD_DIM, (h + 1) * HEAD_DIM)
                    o = head(o_pair, h % 2)
                    o = o * lax.rsqrt(jnp.mean(o * o, axis=-1, keepdims=True) + EPS)
                    o = o * gn_ref[:, sl] * og_ref[s, rows, sl].astype(F32)
                    o_ref[s, rows, sl] = o.astype(BF16)
        return carry

    lax.fori_loop(0, n_chunks, chunk_body, 0)

    @pl.when(tb == pl.num_programs(1) - 1)
    def _():
        for s in seqs:
            for h in range(HEADS):
                s_ref[s, h] = st_scr[s, h].T


def _hgrn_prompt(z3, lf3, gn, acat, lvl, tb, nb):
    bsz, t, _ = z3.shape

    def zspec(col):
        return pl.BlockSpec((nb, tb, D_MODEL), lambda b, i: (b, i, col))

    return pl.pallas_call(
        functools.partial(_hgrn_prompt_kernel, n_chunks=tb // CHUNK),
        grid=(bsz // nb, t // tb),
        in_specs=[
            zspec(0), zspec(1), zspec(2), zspec(3),
            pl.BlockSpec((nb, tb, D_MODEL), lambda b, i: (b, i, 0)),
            pl.BlockSpec((1, D_MODEL), lambda b, i: (0, 0)),
            pl.BlockSpec(acat.shape, lambda b, i: (0, 0)),
            pl.BlockSpec(lvl.shape, lambda b, i: (0, 0)),
        ],
        out_specs=[
            pl.BlockSpec((nb, tb, D_MODEL), lambda b, i: (b, i, 0)),
            pl.BlockSpec((nb, HEADS, HEAD_DIM, HEAD_DIM), lambda b, i: (b, 0, 0, 0)),
        ],
        out_shape=[
            jax.ShapeDtypeStruct((bsz, t, D_MODEL), BF16),
            jax.ShapeDtypeStruct((bsz, HEADS, HEAD_DIM, HEAD_DIM), F32),
        ],
        scratch_shapes=[pltpu.VMEM((nb, HEADS, HEAD_DIM, HEAD_DIM), F32)],
        compiler_params=_cparams(("parallel", "arbitrary")),
        name="hgrn_prompt",
    )(z3, z3, z3, z3, lf3, gn.reshape(1, D_MODEL), acat, lvl)


def _hgrn_sample_kernel(q_ref, k_ref, v_ref, og_ref, lf_ref, gn_ref, s_ref, o_ref, so_ref, o_scr,
                        *, bb):
    n = bb * HEADS
    qt = q_ref[...].reshape(n, HEAD_DIM).T
    kt = k_ref[...].reshape(n, HEAD_DIM).T
    ft = jnp.exp(lf_ref[...].reshape(n, HEAD_DIM)).T
    for b in range(bb):
        for h in range(HEADS):
            c = b * HEADS + h
            s_new = ft[:, c:c + 1] * s_ref[b, h] + kt[:, c:c + 1] * v_ref[b, h:h + 1, :]
            so_ref[b, h] = s_new
            o_scr[c:c + 1, :] = jnp.sum(qt[:, c:c + 1] * s_new, axis=0, keepdims=True)
    o = o_scr[...]
    o = o * lax.rsqrt(jnp.mean(o * o, axis=-1, keepdims=True) + EPS)
    o = o * gn_ref[...] * og_ref[...].reshape(n, HEAD_DIM)
    o_ref[...] = o.astype(BF16)


def _hgrn_sample(z, lf, gn, states, layer, bb):
    n = z.shape[0]
    z3 = z.reshape(n, N_GROUPS_OUT * HEADS, HEAD_DIM)
    lf3 = lf.reshape(n, HEADS, HEAD_DIM)
    gn_rows = jnp.tile(gn.reshape(HEADS, HEAD_DIM), (bb, 1))
    state_block = (bb, HEADS, HEAD_DIM, HEAD_DIM)

    def zspec(col):
        return pl.BlockSpec((bb, HEADS, HEAD_DIM), lambda i: (i, col, 0))

    o, s_new = pl.pallas_call(
        functools.partial(_hgrn_sample_kernel, bb=bb),
        grid=(n // bb,),
        in_specs=[
            zspec(0), zspec(1), zspec(2), zspec(3),
            pl.BlockSpec((bb, HEADS, HEAD_DIM), lambda i: (i, 0, 0)),
            pl.BlockSpec((bb * HEADS, HEAD_DIM), lambda i: (0, 0)),
            pl.BlockSpec((None,) + state_block, lambda i: (layer, i, 0, 0, 0)),
        ],
        out_specs=[
            pl.BlockSpec((bb * HEADS, HEAD_DIM), lambda i: (i, 0)),
            pl.BlockSpec(state_block, lambda i: (i, 0, 0, 0)),
        ],
        out_shape=[
            jax.ShapeDtypeStruct((n * HEADS, HEAD_DIM), BF16),
            jax.ShapeDtypeStruct(states.shape[1:], F32),
        ],
        scratch_shapes=[pltpu.VMEM((bb * HEADS, HEAD_DIM), F32)],
        compiler_params=_cparams(("parallel",)),
        name="hgrn_sample",
    )(z3, z3, z3, z3, lf3, gn_rows, states)
    return o.reshape(n, D_MODEL), s_new


CONV_HIST = 32
CONV_ROWS = 32


def _conv_prompt_kernel(u_ref, wb_ref, bias_ref, y_ref, buf, shifted, *, tc):
    @pl.when(pl.program_id(1) == 0)
    def _():
        buf[0:CONV_HIST, :] = jnp.zeros((CONV_HIST, D_MODEL), F32)

    buf[CONV_HIST:CONV_HIST + tc, :] = u_ref[...]
    lead = CONV_HIST - (CONV_W - 1)
    n_shift = shifted.shape[1]
    for s in range(1, 8):
        shifted[s - 1] = buf[s:s + n_shift, :]

    def rows_body(r, carry):
        base = pl.multiple_of(r * CONV_ROWS, CONV_ROWS)
        accs = [jnp.broadcast_to(bias_ref[...], (8, D_MODEL)) for _ in range(CONV_ROWS // 8)]
        for j in range(CONV_W):
            wj = wb_ref[j * 8:(j + 1) * 8, :]
            p, s = divmod(lead + j, 8)
            for a in range(CONV_ROWS // 8):
                rows = pl.ds(base + 8 * (a + p), 8)
                tap = buf[rows, :] if s == 0 else shifted[s - 1, rows, :]
                accs[a] = accs[a] + wj * tap
        for a in range(CONV_ROWS // 8):
            y_ref[pl.ds(base + a * 8, 8), :] = accs[a]
        return carry

    lax.fori_loop(0, tc // CONV_ROWS, rows_body, 0)
    buf[0:CONV_HIST, :] = buf[tc:tc + CONV_HIST, :]


def _conv_prompt(z3, conv_w, conv_b, tc):
    bsz, t, _ = z3.shape
    wb = jnp.repeat(conv_w, 8, axis=0)
    return pl.pallas_call(
        functools.partial(_conv_prompt_kernel, tc=tc),
        grid=(bsz, t // tc),
        in_specs=[
            pl.BlockSpec((None, tc, D_MODEL), lambda b, i: (b, i, 0)),
            pl.BlockSpec((CONV_W * 8, D_MODEL), lambda b, i: (0, 0)),
            pl.BlockSpec((1, D_MODEL), lambda b, i: (0, 0)),
        ],
        out_specs=pl.BlockSpec((None, tc, D_MODEL), lambda b, i: (b, i, 0)),
        out_shape=jax.ShapeDtypeStruct((bsz, t, D_MODEL), F32),
        scratch_shapes=[pltpu.VMEM((CONV_HIST + tc, D_MODEL), F32),
                        pltpu.VMEM((7, CONV_HIST + tc - 8, D_MODEL), F32)],
        compiler_params=_cparams(("parallel", "arbitrary")),
        name="conv_prompt",
    )(z3, wb, conv_b.reshape(1, D_MODEL))


def _conv_sample_kernel(u_ref, w_ref, bias_ref, s_ref, y_ref, so_ref, *, bb):
    hist = CONV_W - 1
    for b in range(bb):
        u = u_ref[b:b + 1, :]
        y = jnp.sum(s_ref[b] * w_ref[0:hist, :], axis=0, keepdims=True)
        y_ref[b:b + 1, :] = y + w_ref[hist:CONV_W, :] * u + bias_ref[...]
        so_ref[b, 0:hist - 1, :] = s_ref[b, 1:hist, :]
        so_ref[b, hist - 1:hist, :] = u


def _conv_sample(z, states, layer, conv_w, conv_b, bb):
    n = z.shape[0]
    state_block = (bb, CONV_W - 1, D_MODEL)
    return pl.pallas_call(
        functools.partial(_conv_sample_kernel, bb=bb),
        grid=(n // bb,),
        in_specs=[
            pl.BlockSpec((bb, D_MODEL), lambda i: (i, 0)),
            pl.BlockSpec((CONV_W, D_MODEL), lambda i: (0, 0)),
            pl.BlockSpec((1, D_MODEL), lambda i: (0, 0)),
            pl.BlockSpec((None,) + state_block, lambda i: (layer, i, 0, 0)),
        ],
        out_specs=[pl.BlockSpec((bb, D_MODEL), lambda i: (i, 0)),
                   pl.BlockSpec(state_block, lambda i: (i, 0, 0))],
        out_shape=[jax.ShapeDtypeStruct((n, D_MODEL), F32),
                   jax.ShapeDtypeStruct(states.shape[1:], F32)],
        compiler_params=_cparams(("parallel",)),
        name="conv_sample",
    )(z, conv_w, conv_b.reshape(1, D_MODEL), states)


def _merge_kernel(yc_ref, oa_ref, ga_ref, gb_ref, x_ref, g1_ref, sh2_ref, sc2_ref,
                  lng_ref, lnb_ref, n2g_ref, wco_ref, whg_ref, wo_ref, x1_ref, h2_ref):
    yc = yc_ref[...]
    xc = yc - jnp.mean(yc, axis=-1, keepdims=True)
    vln = xc * lax.rsqrt(jnp.mean(xc * xc, axis=-1, keepdims=True) + EPS)
    va = _silu(vln * lng_ref[...] + lnb_ref[...]).astype(BF16)
    y_b = jnp.dot(va, wco_ref[...], preferred_element_type=F32)
    y_a = jnp.dot(oa_ref[...], whg_ref[...], preferred_element_type=F32)
    mixed = (ga_ref[...] * y_a + gb_ref[...] * y_b).astype(BF16)
    x1 = x_ref[...] + g1_ref[...] * jnp.dot(mixed, wo_ref[...], preferred_element_type=F32)
    x1_ref[...] = x1
    y = x1 * lax.rsqrt(jnp.mean(x1 * x1, axis=-1, keepdims=True) + EPS) * n2g_ref[...]
    h2_ref[...] = (y * (1.0 + sc2_ref[...]) + sh2_ref[...]).astype(h2_ref.dtype)


def _merge(yc, oa, z, x, mod, ln_g, ln_b, n2_g, wco, whg, wo, layer, tm, rows_per_mod,
           h2_dtype=BF16):
    m = x.shape[0]
    mod_rows = mod.shape[1]

    def mod_spec(col):
        return pl.BlockSpec((None, mod_rows, D_MODEL),
                            lambda i: ((i * tm) // rows_per_mod, 0, col))

    def row_spec(col=0):
        return pl.BlockSpec((tm, D_MODEL), lambda i: (i, col))

    vec_spec = pl.BlockSpec((1, D_MODEL), lambda i: (0, 0))
    w_spec = pl.BlockSpec((None, D_MODEL, D_MODEL), lambda i: (layer, 0, 0))
    return pl.pallas_call(
        _merge_kernel,
        grid=(m // tm,),
        in_specs=[row_spec(), row_spec(), row_spec(4), row_spec(5), row_spec(),
                  mod_spec(2), mod_spec(3), mod_spec(4),
                  vec_spec, vec_spec, vec_spec, w_spec, w_spec, w_spec],
        out_specs=[row_spec(), row_spec()],
        out_shape=[jax.ShapeDtypeStruct((m, D_MODEL), F32),
                   jax.ShapeDtypeStruct((m, D_MODEL), h2_dtype)],
        compiler_params=_cparams(("parallel",)),
        name="merge",
    )(yc, oa, z, z, x, mod, mod, mod,
      ln_g.reshape(1, D_MODEL), ln_b.reshape(1, D_MODEL), n2_g.reshape(1, D_MODEL),
      wco, whg, wo)


def _swiglu_rows(x, wg_ref, wu_ref, wd_ref, tf, between=None):
    acc = None
    for j in range(D_FF // tf):
        if between is not None:
            between(j)
        cols = slice(j * tf, (j + 1) * tf)
        a = _silu(jnp.dot(x, wg_ref[:, cols], preferred_element_type=F32))
        a = a * jnp.dot(x, wu_ref[:, cols], preferred_element_type=F32)
        d = jnp.dot(a.astype(BF16), wd_ref[cols, :], preferred_element_type=F32)
        acc = d if acc is None else acc + d
    return acc


def _ffn_kernel(h_ref, x1_ref, g2_ref, wg_ref, wu_ref, wd_ref, o_ref, *, tf):
    o_ref[...] = x1_ref[...] + g2_ref[...] * _swiglu_rows(h_ref[...], wg_ref, wu_ref, wd_ref, tf)


def _ffn(h2, x1, mod, wg, wu, wd, tm, tf, rows_per_mod):
    m = x1.shape[0]
    mod_rows = mod.shape[1]
    w_up_spec = pl.BlockSpec((None, D_MODEL, D_FF), lambda i: (0, 0, 0), pipeline_mode=pl.Buffered(1))
    return pl.pallas_call(
        functools.partial(_ffn_kernel, tf=tf),
        grid=(m // tm,),
        in_specs=[
            pl.BlockSpec((tm, D_MODEL), lambda i: (i, 0)),
            pl.BlockSpec((tm, D_MODEL), lambda i: (i, 0)),
            pl.BlockSpec((None, mod_rows, D_MODEL), lambda i: ((i * tm) // rows_per_mod, 0, 5)),
            w_up_spec, w_up_spec,
            pl.BlockSpec((None, D_FF, D_MODEL), lambda i: (0, 0, 0), pipeline_mode=pl.Buffered(1)),
        ],
        out_specs=pl.BlockSpec((tm, D_MODEL), lambda i: (i, 0)),
        out_shape=jax.ShapeDtypeStruct((m, D_MODEL), F32),
        compiler_params=_cparams(("parallel",)),
        name="ffn",
    )(h2, x1, mod, wg, wu, wd)


def _top2(logits, lane):
    neg = jnp.float32(-jnp.inf)
    logits = jnp.where(lane < N_EXPERTS, logits, neg)
    v1 = jnp.max(logits, axis=-1, keepdims=True)
    i1 = jnp.min(jnp.where(logits == v1, lane, LANES), axis=-1, keepdims=True)
    rest = jnp.where(lane == i1, neg, logits)
    v2 = jnp.max(rest, axis=-1, keepdims=True)
    i2 = jnp.min(jnp.where(rest == v2, lane, LANES), axis=-1, keepdims=True)
    e2 = jnp.exp(v2 - v1)
    w1 = 1.0 / (1.0 + e2)
    return i1, i2, w1, e2 * w1


def _route_kernel(h_ref, rw_ref, r_ref):
    lane = lax.broadcasted_iota(jnp.int32, r_ref.shape, 1)
    logits = jnp.dot(h_ref[...].astype(BF16), rw_ref[...], preferred_element_type=F32)
    i1, i2, w1, w2 = _top2(logits, lane)
    r_ref[...] = (jnp.where(lane == 0, i1.astype(F32), 0.0) + jnp.where(lane == 1, i2.astype(F32), 0.0)
                  + jnp.where(lane == 2, w1, 0.0) + jnp.where(lane == 3, w2, 0.0))


def _route(h2, router_pad, tm):
    m = h2.shape[0]
    return pl.pallas_call(
        _route_kernel,
        grid=(m // tm,),
        in_specs=[pl.BlockSpec((tm, D_MODEL), lambda i: (i, 0)),
                  pl.BlockSpec((D_MODEL, LANES), lambda i: (0, 0))],
        out_specs=pl.BlockSpec((tm, LANES), lambda i: (i, 0)),
        out_shape=jax.ShapeDtypeStruct((m, LANES), F32),
        compiler_params=_cparams(("parallel",)),
        name="route",
    )(h2, router_pad)


def _start_row_gather(idx_ref, n_rows, src_hbm, dst, sem):
    def body(r, carry):
        pltpu.make_async_copy(src_hbm.at[pl.ds(idx_ref[r], 1), :], dst.at[pl.ds(r, 1), :], sem).start()
        return carry
    lax.fori_loop(0, n_rows, body, 0, unroll=8)


def _moe_dispatch_kernel(lo_ref, hi_ref, pos_ref, h_ref, x_hbm, stage, zero_row, sem, pad_sem):
    i = pl.program_id(0)
    last = pl.num_programs(0) - 1
    slot = lax.rem(i, 2)
    tb = stage.shape[1]

    def wait_block(s):
        for _ in range(2):
            pltpu.make_async_copy(stage.at[s], stage.at[s], sem.at[s]).wait()

    @pl.when(i >= 2)
    def _():
        wait_block(slot)

    stage[slot] = h_ref[...]

    def body(r, carry):
        row = stage.at[slot, pl.ds(r, 1), :]
        pltpu.make_async_copy(row, x_hbm.at[pl.ds(pos_ref[0, r], 1), :], sem.at[slot]).start()
        pltpu.make_async_copy(row, x_hbm.at[pl.ds(pos_ref[0, tb + r], 1), :], sem.at[slot]).start()
        return carry
    lax.fori_loop(0, tb, body, 0, unroll=4)

    @pl.when(i == last)
    def _():
        zero_row[...] = jnp.zeros_like(zero_row)
        pad_copy = lambda r: pltpu.make_async_copy(zero_row.at[pl.ds(0, 1), :],
                                                   x_hbm.at[pl.ds(r, 1), :], pad_sem.at[0])

        def start(r, carry):
            pad_copy(r).start()
            return carry

        def wait(r, carry):
            pad_copy(r).wait()
            return carry

        for e in range(lo_ref.shape[0]):
            lax.fori_loop(lo_ref[e], hi_ref[e], start, 0)
        for e in range(lo_ref.shape[0]):
            lax.fori_loop(lo_ref[e], hi_ref[e], wait, 0)
        wait_block(1 - slot)
        wait_block(slot)


def _moe_dispatch(h2, pos_blocks, pad_lo, pad_hi, n_rows):
    n_blocks = pos_blocks.shape[0]
    tb = pos_blocks.shape[2] // 2
    assert n_blocks >= 2
    grid_spec = pltpu.PrefetchScalarGridSpec(
        num_scalar_prefetch=2,
        grid=(n_blocks,),
        in_specs=[
            pl.BlockSpec((None, 1, 2 * tb), lambda i, lo, hi: (i, 0, 0), memory_space=pltpu.SMEM),
            pl.BlockSpec((tb, D_MODEL), lambda i, lo, hi: (i, 0)),
        ],
        out_specs=pl.BlockSpec(memory_space=pl.ANY),
        scratch_shapes=[pltpu.VMEM((2, tb, D_MODEL), F32), pltpu.VMEM((8, D_MODEL), F32),
                        pltpu.SemaphoreType.DMA((2,)), pltpu.SemaphoreType.DMA((1,))],
    )
    return pl.pallas_call(
        _moe_dispatch_kernel,
        grid_spec=grid_spec,
        out_shape=jax.ShapeDtypeStruct((n_rows, D_MODEL), F32),
        compiler_params=_cparams(("arbitrary",)),
        name="moe_dispatch",
    )(pad_lo, pad_hi, pos_blocks, h2)


def _moe_tiles_kernel(te_ref, na_ref, x_ref, wg_ref, wu_ref, wd_ref, y_ref, *, tf):
    @pl.when(pl.program_id(0) < na_ref[0])
    def _():
        y_ref[...] = _swiglu_rows(x_ref[...].astype(BF16), wg_ref, wu_ref, wd_ref, tf)

    @pl.when(pl.program_id(0) >= na_ref[0])
    def _():
        y_ref[...] = jnp.zeros_like(y_ref)


def _moe_tiles(x_sorted, tile_expert, n_active, wg, wu, wd, tm, tf):
    n_tiles = x_sorted.shape[0] // tm
    w_up_spec = pl.BlockSpec((None, None, D_MODEL, D_FF), lambda i, te, na: (0, te[i], 0, 0))
    grid_spec = pltpu.PrefetchScalarGridSpec(
        num_scalar_prefetch=2,
        grid=(n_tiles,),
        in_specs=[
            pl.BlockSpec((tm, D_MODEL), lambda i, te, na: (jnp.minimum(i, na[0] - 1), 0)),
            w_up_spec, w_up_spec,
            pl.BlockSpec((None, None, D_FF, D_MODEL), lambda i, te, na: (0, te[i], 0, 0)),
        ],
        out_specs=pl.BlockSpec((tm, D_MODEL), lambda i, te, na: (i, 0)),
    )
    return pl.pallas_call(
        functools.partial(_moe_tiles_kernel, tf=tf),
        grid_spec=grid_spec,
        out_shape=jax.ShapeDtypeStruct((n_tiles * tm, D_MODEL), F32),
        compiler_params=pltpu.CompilerParams(dimension_semantics=("arbitrary",),
                                             vmem_limit_bytes=MOE_VMEM_LIMIT),
        name="moe_tiles",
    )(tile_expert, n_active, x_sorted, wg, wu, wd)


def _moe_combine_kernel(pos_ref, pos_next_ref, y_hbm, r_ref, x1_ref, g2_ref, fg_ref, o_ref,
                        buf, sem):
    i = pl.program_id(0)
    slot = lax.rem(i, 2)
    n_rows = buf.shape[1]

    @pl.when(i == 0)
    def _():
        _start_row_gather(pos_ref.at[0], n_rows, y_hbm, buf.at[0], sem.at[0])

    @pl.when(i + 1 < pl.num_programs(0))
    def _():
        _start_row_gather(pos_next_ref.at[0], n_rows, y_hbm, buf.at[1 - slot], sem.at[1 - slot])

    pltpu.make_async_copy(buf.at[slot], buf.at[slot], sem.at[slot]).wait()
    tc = n_rows // 2
    lane = lax.broadcasted_iota(jnp.int32, r_ref.shape, 1)
    r = r_ref[...]
    w1 = jnp.sum(jnp.where(lane == 2, r, 0.0), axis=-1, keepdims=True)
    w2 = jnp.sum(jnp.where(lane == 3, r, 0.0), axis=-1, keepdims=True)
    f = w1 * buf[slot, 0:tc, :] + w2 * buf[slot, tc:n_rows, :]
    x2 = x1_ref[...] + g2_ref[...] * f
    o_ref[...] = x2 * lax.rsqrt(jnp.mean(x2 * x2, axis=-1, keepdims=True) + EPS) * fg_ref[...]


def _moe_combine(y_sorted, pos, route, x1, mod, final_g, tc, rows_per_mod):
    m = x1.shape[0]
    n_blocks = m // tc
    mod_rows = mod.shape[1]
    pos_spec = lambda f: pl.BlockSpec((None, 1, 2 * tc), f, memory_space=pltpu.SMEM)
    return pl.pallas_call(
        _moe_combine_kernel,
        grid=(n_blocks,),
        in_specs=[
            pos_spec(lambda i: (i, 0, 0)),
            pos_spec(lambda i: (jnp.minimum(i + 1, n_blocks - 1), 0, 0)),
            pl.BlockSpec(memory_space=pl.ANY),
            pl.BlockSpec((tc, LANES), lambda i: (i, 0)),
            pl.BlockSpec((tc, D_MODEL), lambda i: (i, 0)),
            pl.BlockSpec((None, mod_rows, D_MODEL), lambda i: ((i * tc) // rows_per_mod, 0, 5)),
            pl.BlockSpec((1, D_MODEL), lambda i: (0, 0)),
        ],
        out_specs=pl.BlockSpec((tc, D_MODEL), lambda i: (i, 0)),
        out_shape=jax.ShapeDtypeStruct((m, D_MODEL), F32),
        scratch_shapes=[pltpu.VMEM((2, 2 * tc, D_MODEL), F32), pltpu.SemaphoreType.DMA((2,))],
        compiler_params=_cparams(("arbitrary",)),
        name="moe_combine",
    )(pos, pos, y_sorted, route, x1, mod, final_g.reshape(1, D_MODEL))


def _moe_routed(h2, x1, mod, router_pad, wg, wu, wd, final_g, tm, tf, tc, rows_per_mod):
    n = h2.shape[0]
    route = _route(h2, router_pad, 512)
    expert = route[:, :2].astype(jnp.int32).reshape(2 * n)
    onehot = (expert[:, None] == jnp.arange(N_EXPERTS, dtype=jnp.int32)[None, :]).astype(jnp.int32)
    running = jnp.cumsum(onehot, axis=0)
    rank = jnp.sum((running - onehot) * onehot, axis=1)
    count = running[-1]
    padded = ((count + tm - 1) // tm) * tm
    group_end = jnp.cumsum(padded)
    group_start = group_end - padded
    pos = group_start[expert] + rank
    n_tiles = (2 * n) // tm + N_EXPERTS
    n_rows = n_tiles * tm
    tile_start = jnp.arange(n_tiles, dtype=jnp.int32) * tm
    tile_expert = jnp.minimum(jnp.sum(tile_start[:, None] >= group_end[None, :], axis=1),
                              N_EXPERTS - 1).astype(jnp.int32)
    n_active = (group_end[-1:] // tm).astype(jnp.int32)
    pad_lo = jnp.concatenate([group_start + count, group_end[-1:]]).astype(jnp.int32)
    pad_hi = jnp.concatenate([group_end, jnp.full((1,), n_rows)]).astype(jnp.int32)
    pos_blocks = pos.reshape(n // tc, tc, 2).transpose(0, 2, 1).reshape(n // tc, 1, 2 * tc)

    x_sorted = _moe_dispatch(h2, pos_blocks, pad_lo, pad_hi, n_rows)
    y_sorted = _moe_tiles(x_sorted, tile_expert, n_active, wg, wu, wd, tm, tf)
    return _moe_combine(y_sorted, pos_blocks, route, x1, mod, final_g, tc, rows_per_mod)


def _moe_kernel(h_ref, x1_ref, g2_ref, rw_ref, wg_ref, wu_ref, wd_ref, fg_ref, o_ref,
                acc, comb):
    e = pl.program_id(1)
    j = pl.program_id(2)
    lane = lax.broadcasted_iota(jnp.int32, comb.shape, 1)

    @pl.when((e == 0) & (j == 0))
    def _():
        acc[...] = jnp.zeros_like(acc)
        i1, i2, w1, w2 = _top2(jnp.dot(h_ref[...], rw_ref[...], preferred_element_type=F32), lane)
        comb[...] = jnp.where(lane == i1, w1, 0.0) + jnp.where(lane == i2, w2, 0.0)

    ce = jnp.sum(jnp.where(lane == e, comb[...], 0.0), axis=-1, keepdims=True)
    h = h_ref[...]
    a = _silu(jnp.dot(h, wg_ref[...], preferred_element_type=F32))
    a = a * jnp.dot(h, wu_ref[...], preferred_element_type=F32) * ce
    acc[...] += jnp.dot(a.astype(BF16), wd_ref[...], preferred_element_type=F32)

    @pl.when((e == pl.num_programs(1) - 1) & (j == pl.num_programs(2) - 1))
    def _():
        x2 = x1_ref[...] + g2_ref[...] * acc[...]
        o_ref[...] = x2 * lax.rsqrt(jnp.mean(x2 * x2, axis=-1, keepdims=True) + EPS) * fg_ref[...]


def _moe(h2, x1, mod, router_pad, wg, wu, wd, final_g, tm, tf, rows_per_mod):
    m = x1.shape[0]
    mod_rows = mod.shape[1]
    return pl.pallas_call(
        _moe_kernel,
        grid=(m // tm, N_EXPERTS, D_FF // tf),
        in_specs=[
            pl.BlockSpec((tm, D_MODEL), lambda i, e, j: (i, 0)),
            pl.BlockSpec((tm, D_MODEL), lambda i, e, j: (i, 0)),
            pl.BlockSpec((None, mod_rows, D_MODEL),
                         lambda i, e, j: ((i * tm) // rows_per_mod, 0, 5)),
            pl.BlockSpec((D_MODEL, LANES), lambda i, e, j: (0, 0)),
            pl.BlockSpec((None, None, D_MODEL, tf), lambda i, e, j: (0, e, 0, j)),
            pl.BlockSpec((None, None, D_MODEL, tf), lambda i, e, j: (0, e, 0, j)),
            pl.BlockSpec((None, None, tf, D_MODEL), lambda i, e, j: (0, e, j, 0)),
            pl.BlockSpec((1, D_MODEL), lambda i, e, j: (0, 0)),
        ],
        out_specs=pl.BlockSpec((tm, D_MODEL), lambda i, e, j: (i, 0)),
        out_shape=jax.ShapeDtypeStruct((m, D_MODEL), F32),
        scratch_shapes=[pltpu.VMEM((tm, D_MODEL), F32), pltpu.VMEM((tm, LANES), F32)],
        compiler_params=_cparams(("parallel", "arbitrary", "arbitrary")),
        name="moe",
    )(h2, x1, mod, router_pad, wg, wu, wd, final_g.reshape(1, D_MODEL))


def kernel(x_prompt, x_sample, state_hgrn, state_conv, c_prompt, c_sample, ada_w, ada_b, norm1_g, norm2_g, w_in, lb_logits, hg_norm_g, w_hg_out, conv_w, conv_b, conv_ln_g, conv_ln_b, w_conv_out, w_o, ffn_w_gate, ffn_w_up, ffn_w_down, router_w, moe_w_gate, moe_w_up, moe_w_down, final_norm_g):
    bsz, seq, _ = x_prompt.shape
    n_s = x_sample.shape[0]
    hist = CONV_W - 1

    w_in_b = w_in.astype(BF16)
    whg_b = w_hg_out.astype(BF16)
    wco_b = w_conv_out.astype(BF16)
    wo_b = w_o.astype(BF16)
    ffn_g, ffn_u, ffn_d = (w.astype(BF16) for w in (ffn_w_gate, ffn_w_up, ffn_w_down))
    moe_g, moe_u, moe_d = (w.astype(BF16) for w in (moe_w_gate, moe_w_up, moe_w_down))
    router_pad = jnp.pad(router_w[0], ((0, 0), (0, LANES - N_EXPERTS))).astype(BF16)
    acat, lvl = _hgrn_constants()

    mod_all = _ada(jnp.concatenate([c_prompt, c_sample], axis=0), ada_w, ada_b)

    xp = x_prompt.reshape(bsz * seq, D_MODEL)
    xs = x_sample.reshape(n_s, D_MODEL)
    tm_p = 256
    hgrn_p, conv_p, hgrn_s, conv_s = [], [], [], []
    for l in range(DEPTH):
        mod_p = mod_all[l, :bsz].reshape(bsz, 1, 6 * D_MODEL)
        mod_s = mod_all[l, bsz:].reshape(1, n_s, 6 * D_MODEL)

        zp, up, lfp = _inproj(xp, mod_p, norm1_g[l], lb_logits, w_in_b, l, tm_p, seq, BF16)
        zs, us, lfs = _inproj(xs, mod_s, norm1_g[l], lb_logits, w_in_b, l, n_s, n_s, F32)

        zp3 = zp.reshape(bsz, seq, N_GROUPS_OUT * D_MODEL)
        up3 = up.reshape(bsz, seq, D_MODEL)
        oa_p, s_p = _hgrn_prompt(zp3, lfp.reshape(bsz, seq, D_MODEL), hg_norm_g[l], acat, lvl, 256, 2)
        oa_s, s_s = _hgrn_sample(zs, lfs, hg_norm_g[l], state_hgrn, l, 8)

        yc_p = _conv_prompt(up3, conv_w[l], conv_b[l], 256)
        yc_s, c_s = _conv_sample(us, state_conv, l, conv_w[l], conv_b[l], 8)
        c_p = up3[:, seq - hist:, :]

        merge_w = (conv_ln_g[l], conv_ln_b[l], norm2_g[l], wco_b, whg_b, wo_b, l)
        dense = l % 2 == 0
        x1p, h2p = _merge(yc_p.reshape(bsz * seq, D_MODEL), oa_p.reshape(bsz * seq, D_MODEL),
                          zp, xp, mod_p, *merge_w, 256, seq, BF16 if dense else F32)
        x1s, h2s = _merge(yc_s, oa_s, zs, xs, mod_s, *merge_w, n_s, n_s)

        if dense:
            xp = _ffn(h2p, x1p, mod_p, ffn_g, ffn_u, ffn_d, 512, 256, seq)
            xs = _ffn(h2s, x1s, mod_s, ffn_g, ffn_u, ffn_d, n_s, 256, n_s)
        else:
            xp = _moe_routed(h2p, x1p, mod_p, router_pad, moe_g, moe_u, moe_d, final_norm_g,
                             512, 256, 256, seq)
            xs = _moe(h2s, x1s, mod_s, router_pad, moe_g, moe_u, moe_d, final_norm_g, n_s, 256, n_s)

        hgrn_p.append(s_p)
        conv_p.append(c_p)
        hgrn_s.append(s_s)
        conv_s.append(c_s)

    return (xp.reshape(bsz, seq, D_MODEL), xs.reshape(n_s, 1, D_MODEL),
            jnp.stack(hgrn_p), jnp.stack(conv_p), jnp.stack(hgrn_s), jnp.stack(conv_s))
```

```python
import functools

import numpy as np
import jax
import jax.numpy as jnp
from jax import lax
from jax.experimental import pallas as pl
from jax.experimental.pallas import tpu as pltpu

F32 = jnp.float32
BF16 = jnp.bfloat16

D_MODEL = 1024
DEPTH = 2
HEADS = 8
HEAD_DIM = 128
CHUNK = 64
CONV_W = 31
N_EXPERTS = 8
D_FF = 2816
EPS = 1e-6
LOG2_E = 1.4426950408889634
LANES = 128
N_GROUPS_OUT = 6
VMEM_LIMIT = 48 * 1024 * 1024
MOE_VMEM_LIMIT = 56 * 1024 * 1024

LEVEL_WIDTHS = tuple(CHUNK >> (i + 1) for i in range(CHUNK.bit_length() - 1))


def _sigmoid(x):
    return 1.0 / (1.0 + jnp.exp(-x))


def _silu(x):
    return x * _sigmoid(x)


def _cparams(sem):
    return pltpu.CompilerParams(dimension_semantics=sem, vmem_limit_bytes=VMEM_LIMIT)


def _ada_kernel(c_ref, w_ref, b_ref, o_ref):
    a = _silu(c_ref[...]).astype(BF16)
    o_ref[...] = jnp.dot(a, w_ref[...].astype(BF16), preferred_element_type=F32) + b_ref[...]


def _ada(c_all, ada_w, ada_b):
    n = c_all.shape[0]
    tn = 1024
    return pl.pallas_call(
        _ada_kernel,
        grid=(DEPTH, 6 * D_MODEL // tn),
        in_specs=[
            pl.BlockSpec((n, D_MODEL), lambda l, j: (0, 0)),
            pl.BlockSpec((None, D_MODEL, tn), lambda l, j: (l, 0, j)),
            pl.BlockSpec((None, 1, tn), lambda l, j: (l, 0, j)),
        ],
        out_specs=pl.BlockSpec((None, n, tn), lambda l, j: (l, 0, j)),
        out_shape=jax.ShapeDtypeStruct((DEPTH, n, 6 * D_MODEL), F32),
        compiler_params=_cparams(("parallel", "parallel")),
        name="ada",
    )(c_all, ada_w, ada_b.reshape(DEPTH, 1, 6 * D_MODEL))


def _inproj_kernel(x_ref, sh_ref, sc_ref, g_ref, lbl_ref, w_ref, z_ref, u_ref, lf_ref, *, layer):
    x = x_ref[...]
    y = x * lax.rsqrt(jnp.mean(x * x, axis=-1, keepdims=True) + EPS) * g_ref[...]
    h = (y * (1.0 + sc_ref[...]) + sh_ref[...]).astype(BF16)

    def proj(group):
        return jnp.dot(h, w_ref[:, group * D_MODEL:(group + 1) * D_MODEL],
                       preferred_element_type=F32)

    def put(group, value):
        z_ref[:, group * D_MODEL:(group + 1) * D_MODEL] = value.astype(z_ref.dtype)

    put(0, _silu(proj(0)))
    lbl = lbl_ref[...]
    e = jnp.exp(lbl - jnp.max(lbl, axis=0, keepdims=True))
    p = e / jnp.sum(e, axis=0, keepdims=True)
    cum = p[0:1]
    for i in range(1, layer + 1):
        cum = cum + p[i:i + 1]
    lb = cum - p[0:1]
    zf = proj(1)
    put(1, (1.0 - lb) * _sigmoid(-zf))
    lf_ref[...] = jnp.log(lb + (1.0 - lb) * _sigmoid(zf))
    put(2, proj(2))
    put(3, _silu(proj(3)))
    u_ref[...] = proj(4) * _sigmoid(proj(5))
    put(4, _sigmoid(proj(6)))
    put(5, _sigmoid(proj(7)))


def _inproj(x, mod, norm_g, lb_logits, w_in_b, layer, tm, rows_per_mod, z_dtype):
    m = x.shape[0]
    mod_rows = mod.shape[1]
    n_cols = w_in_b.shape[-1]

    def mod_spec(col):
        return pl.BlockSpec((None, mod_rows, D_MODEL),
                            lambda i: ((i * tm) // rows_per_mod, 0, col))

    return pl.pallas_call(
        functools.partial(_inproj_kernel, layer=layer),
        grid=(m // tm,),
        in_specs=[
            pl.BlockSpec((tm, D_MODEL), lambda i: (i, 0)),
            mod_spec(0), mod_spec(1),
            pl.BlockSpec((1, D_MODEL), lambda i: (0, 0)),
            pl.BlockSpec((DEPTH, D_MODEL), lambda i: (0, 0)),
            pl.BlockSpec((None, D_MODEL, n_cols), lambda i: (layer, 0, 0),
                         pipeline_mode=pl.Buffered(1)),
        ],
        out_specs=[
            pl.BlockSpec((tm, N_GROUPS_OUT * D_MODEL), lambda i: (i, 0)),
            pl.BlockSpec((tm, D_MODEL), lambda i: (i, 0)),
            pl.BlockSpec((tm, D_MODEL), lambda i: (i, 0)),
        ],
        out_shape=[
            jax.ShapeDtypeStruct((m, N_GROUPS_OUT * D_MODEL), z_dtype),
            jax.ShapeDtypeStruct((m, D_MODEL), F32),
            jax.ShapeDtypeStruct((m, D_MODEL), F32),
        ],
        compiler_params=_cparams(("parallel",)),
        name="inproj",
    )(x, mod, mod, norm_g.reshape(1, D_MODEL), lb_logits, w_in_b)


def _hgrn_constants():
    t = np.arange(CHUNK)
    s = t[None, :]
    blocks = [(s <= t[:, None])]
    for w in LEVEL_WIDTHS:
        ref_row = ((t & ~(2 * w - 1)) + w - 1)[:, None]
        is_query = ((t & w) != 0)[:, None]
        blocks.append(np.where(is_query, (s > ref_row) & (s <= t[:, None]),
                               (s > t[:, None]) & (s <= ref_row)))
    cum = np.concatenate(blocks, axis=0).astype(np.float32)
    acat = np.concatenate([cum, cum, cum, np.zeros_like(cum)], axis=1)
    x = t[:, None] ^ t[None, :]
    lvl = np.full((CHUNK, CHUNK), -1, np.int32)
    for li, w in enumerate(LEVEL_WIDTHS):
        lvl[((x // w) == 1) & ((t[:, None] & w) != 0)] = li
    lvl[t[:, None] == t[None, :]] = len(LEVEL_WIDTHS)
    return jnp.asarray(acat, BF16), jnp.asarray(np.tile(lvl, (1, 2)))


def _block_diag(a, b):
    z = jnp.zeros_like(a)
    return jnp.concatenate([jnp.concatenate([a, z], axis=1),
                            jnp.concatenate([z, b], axis=1)], axis=0)


def _hgrn_prompt_kernel(q_ref, k_ref, v_ref, og_ref, lf_ref, gn_ref, acat_ref, lvl_ref,
                        o_ref, s_ref, st_scr, ops_scr, dec_scr, *, n_chunks):
    tb = pl.program_id(1)
    seqs = range(st_scr.shape[0])

    @pl.when(tb == 0)
    def _():
        st_scr[...] = jnp.zeros_like(st_scr)

    lvl = lvl_ref[...]
    trans_b = (((1,), (1,)), ((), ()))
    trans_a = (((0,), (0,)), ((), ()))
    pair_w = 2 * HEAD_DIM
    n_pairs = HEADS // 2
    n_lvl = len(LEVEL_WIDTHS)

    def head(x, h):
        return x[:, h * HEAD_DIM:(h + 1) * HEAD_DIM]

    def pair_rows(x, p):
        return _block_diag(head(x, 2 * p), head(x, 2 * p + 1))

    def lanes(p):
        return slice(p * pair_w, (p + 1) * pair_w)

    def chunk_rows(c):
        return pl.ds(pl.multiple_of(c * CHUNK, CHUNK), CHUNK)

    qe_at, kd_at = 2 * n_lvl, 2 * n_lvl + 1

    def prepare(c, slot):
        rows = chunk_rows(c)
        for s in seqs:
            lf = lf_ref[s, rows, :] * LOG2_E
            hi = lf.astype(BF16)
            r1 = lf - hi.astype(F32)
            mid = r1.astype(BF16)
            lo = (r1 - mid.astype(F32)).astype(BF16)
            lf4 = jnp.concatenate([hi, mid, lo, jnp.zeros_like(hi)], axis=0)
            cums = jnp.dot(acat_ref[...], lf4, preferred_element_type=F32)
            qb = q_ref[s, rows, :]
            kb = k_ref[s, rows, :]
            for li in range(n_lvl):
                e = jnp.exp2(cums[(li + 1) * CHUNK:(li + 2) * CHUNK]).astype(BF16)
                ops_scr[slot, s, li] = qb * e
                ops_scr[slot, s, n_lvl + li] = kb * e
            b = cums[0:CHUNK]
            b_last = b[CHUNK - 1:CHUNK]
            ops_scr[slot, s, qe_at] = qb * jnp.exp2(b).astype(BF16)
            ops_scr[slot, s, kd_at] = kb * jnp.exp2(b_last - b).astype(BF16)
            dec_scr[slot, s] = jnp.broadcast_to(jnp.exp2(b_last), dec_scr.shape[2:])

    def consume(c, slot):
        rows = chunk_rows(c)
        qb = [q_ref[s, rows, :] for s in seqs]
        kb = [k_ref[s, rows, :] for s in seqs]
        v = [v_ref[s, rows, :] for s in seqs]
        att = [[jnp.where(lvl == n_lvl,
                          lax.dot_general(qb[s][:, lanes(p)], pair_rows(kb[s], p), trans_b,
                                          preferred_element_type=F32), 0.0)
                for p in range(n_pairs)] for s in seqs]
        for li in range(n_lvl):
            for s in seqs:
                xq = ops_scr[slot, s, li]
                xk = ops_scr[slot, s, n_lvl + li]
                for p in range(n_pairs):
                    g = lax.dot_general(xq[:, lanes(p)], pair_rows(xk, p), trans_b,
                                        preferred_element_type=F32)
                    att[s][p] = jnp.where(lvl == li, g, att[s][p])
        inter = []
        for s in seqs:
            qe = ops_scr[slot, s, qe_at]
            inter.append([lax.dot_general(
                qe[:, lanes(p)],
                _block_diag(st_scr[s, 2 * p].astype(BF16), st_scr[s, 2 * p + 1].astype(BF16)),
                trans_b, preferred_element_type=F32) for p in range(n_pairs)])
        for s in seqs:
            kd = ops_scr[slot, s, kd_at]
            decay = dec_scr[slot, s, 0:1, :]
            for h in range(HEADS):
                st_scr[s, h] = head(decay, h) * st_scr[s, h] + lax.dot_general(
                    head(v[s], h), head(kd, h), trans_a, preferred_element_type=F32)
        for s in seqs:
            for p in range(n_pairs):
                o_pair = inter[s][p] + jnp.dot(att[s][p].astype(BF16), pair_rows(v[s], p),
                                               preferred_element_type=F32)
                for h in (2 * p, 2 * p + 1):
                    sl = slice(h * HEAD_DIM, (h + 1) * HEAD_DIM)
                    o = head(o_pair, h % 2)
                    o = o * lax.rsqrt(jnp.mean(o * o, axis=-1, keepdims=True) + EPS)
                    o = o * gn_ref[:, sl] * og_ref[s, rows, sl].astype(F32)
                    o_ref[s, rows, sl] = o.astype(BF16)

    assert n_chunks % 2 == 0
    prepare(0, 0)

    def chunk_pair(j, carry):
        prepare(2 * j + 1, 1)
        consume(2 * j, 0)
        prepare(2 * j + 2, 0)
        consume(2 * j + 1, 1)
        return carry

    lax.fori_loop(0, n_chunks // 2 - 1, chunk_pair, 0)
    prepare(n_chunks - 1, 1)
    consume(n_chunks - 2, 0)
    consume(n_chunks - 1, 1)

    @pl.when(tb == pl.num_programs(1) - 1)
    def _():
        for s in seqs:
            for h in range(HEADS):
                s_ref[s, h] = st_scr[s, h].T


def _hgrn_prompt(z3, lf3, gn, acat, lvl, tb, nb):
    bsz, t, _ = z3.shape

    def zspec(col):
        return pl.BlockSpec((nb, tb, D_MODEL), lambda b, i: (b, i, col))

    return pl.pallas_call(
        functools.partial(_hgrn_prompt_kernel, n_chunks=tb // CHUNK),
        grid=(bsz // nb, t // tb),
        in_specs=[
            zspec(0), zspec(1), zspec(2), zspec(3),
            pl.BlockSpec((nb, tb, D_MODEL), lambda b, i: (b, i, 0)),
            pl.BlockSpec((1, D_MODEL), lambda b, i: (0, 0)),
            pl.BlockSpec(acat.shape, lambda b, i: (0, 0)),
            pl.BlockSpec(lvl.shape, lambda b, i: (0, 0)),
        ],
        out_specs=[
            pl.BlockSpec((nb, tb, D_MODEL), lambda b, i: (b, i, 0)),
            pl.BlockSpec((nb, HEADS, HEAD_DIM, HEAD_DIM), lambda b, i: (b, 0, 0, 0)),
        ],
        out_shape=[
            jax.ShapeDtypeStruct((bsz, t, D_MODEL), BF16),
            jax.ShapeDtypeStruct((bsz, HEADS, HEAD_DIM, HEAD_DIM), F32),
        ],
        scratch_shapes=[
            pltpu.VMEM((nb, HEADS, HEAD_DIM, HEAD_DIM), F32),
            pltpu.VMEM((2, nb, 2 * len(LEVEL_WIDTHS) + 2, CHUNK, D_MODEL), BF16),
            pltpu.VMEM((2, nb, 8, D_MODEL), F32),
        ],
        compiler_params=_cparams(("parallel", "arbitrary")),
        name="hgrn_prompt",
    )(z3, z3, z3, z3, lf3, gn.reshape(1, D_MODEL), acat, lvl)


def _hgrn_sample_kernel(q_ref, k_ref, v_ref, og_ref, lf_ref, gn_ref, s_ref, o_ref, so_ref, o_scr,
                        *, bb):
    n = bb * HEADS
    qt = q_ref[...].reshape(n, HEAD_DIM).T
    kt = k_ref[...].reshape(n, HEAD_DIM).T
    ft = jnp.exp(lf_ref[...].reshape(n, HEAD_DIM)).T
    for b in range(bb):
        for h in range(HEADS):
            c = b * HEADS + h
            s_new = ft[:, c:c + 1] * s_ref[b, h] + kt[:, c:c + 1] * v_ref[b, h:h + 1, :]
            so_ref[b, h] = s_new
            o_scr[c:c + 1, :] = jnp.sum(qt[:, c:c + 1] * s_new, axis=0, keepdims=True)
    o = o_scr[...]
    o = o * lax.rsqrt(jnp.mean(o * o, axis=-1, keepdims=True) + EPS)
    o = o * gn_ref[...] * og_ref[...].reshape(n, HEAD_DIM)
    o_ref[...] = o.astype(BF16)


def _hgrn_sample(z, lf, gn, states, layer, bb):
    n = z.shape[0]
    z3 = z.reshape(n, N_GROUPS_OUT * HEADS, HEAD_DIM)
    lf3 = lf.reshape(n, HEADS, HEAD_DIM)
    gn_rows = jnp.tile(gn.reshape(HEADS, HEAD_DIM), (bb, 1))
    state_block = (bb, HEADS, HEAD_DIM, HEAD_DIM)

    def zspec(col):
        return pl.BlockSpec((bb, HEADS, HEAD_DIM), lambda i: (i, col, 0))

    o, s_new = pl.pallas_call(
        functools.partial(_hgrn_sample_kernel, bb=bb),
        grid=(n // bb,),
        in_specs=[
            zspec(0), zspec(1), zspec(2), zspec(3),
            pl.BlockSpec((bb, HEADS, HEAD_DIM), lambda i: (i, 0, 0)),
            pl.BlockSpec((bb * HEADS, HEAD_DIM), lambda i: (0, 0)),
            pl.BlockSpec((None,) + state_block, lambda i: (layer, i, 0, 0, 0)),
        ],
        out_specs=[
            pl.BlockSpec((bb * HEADS, HEAD_DIM), lambda i: (i, 0)),
            pl.BlockSpec(state_block, lambda i: (i, 0, 0, 0)),
        ],
        out_shape=[
            jax.ShapeDtypeStruct((n * HEADS, HEAD_DIM), BF16),
            jax.ShapeDtypeStruct(states.shape[1:], F32),
        ],
        scratch_shapes=[pltpu.VMEM((bb * HEADS, HEAD_DIM), F32)],
        compiler_params=_cparams(("parallel",)),
        name="hgrn_sample",
    )(z3, z3, z3, z3, lf3, gn_rows, states)
    return o.reshape(n, D_MODEL), s_new


CONV_HIST = 32
CONV_ROWS = 32


def _conv_prompt_kernel(u_ref, wb_ref, bias_ref, y_ref, buf, shifted, *, tc):
    @pl.when(pl.program_id(1) == 0)
    def _():
        buf[0:CONV_HIST, :] = jnp.zeros((CONV_HIST, D_MODEL), F32)

    buf[CONV_HIST:CONV_HIST + tc, :] = u_ref[...]
    lead = CONV_HIST - (CONV_W - 1)
    n_shift = shifted.shape[1]
    for s in range(1, 8):
        shifted[s - 1] = buf[s:s + n_shift, :]

    def rows_body(r, carry):
        base = pl.multiple_of(r * CONV_ROWS, CONV_ROWS)
        accs = [jnp.broadcast_to(bias_ref[...], (8, D_MODEL)) for _ in range(CONV_ROWS // 8)]
        for j in range(CONV_W):
            wj = wb_ref[j * 8:(j + 1) * 8, :]
            p, s = divmod(lead + j, 8)
            for a in range(CONV_ROWS // 8):
                rows = pl.ds(base + 8 * (a + p), 8)
                tap = buf[rows, :] if s == 0 else shifted[s - 1, rows, :]
                accs[a] = accs[a] + wj * tap
        for a in range(CONV_ROWS // 8):
            y_ref[pl.ds(base + a * 8, 8), :] = accs[a]
        return carry

    lax.fori_loop(0, tc // CONV_ROWS, rows_body, 0)
    buf[0:CONV_HIST, :] = buf[tc:tc + CONV_HIST, :]


def _conv_prompt(z3, conv_w, conv_b, tc):
    bsz, t, _ = z3.shape
    wb = jnp.repeat(conv_w, 8, axis=0)
    return pl.pallas_call(
        functools.partial(_conv_prompt_kernel, tc=tc),
        grid=(bsz, t // tc),
        in_specs=[
            pl.BlockSpec((None, tc, D_MODEL), lambda b, i: (b, i, 0)),
            pl.BlockSpec((CONV_W * 8, D_MODEL), lambda b, i: (0, 0)),
            pl.BlockSpec((1, D_MODEL), lambda b, i: (0, 0)),
        ],
        out_specs=pl.BlockSpec((None, tc, D_MODEL), lambda b, i: (b, i, 0)),
        out_shape=jax.ShapeDtypeStruct((bsz, t, D_MODEL), F32),
        scratch_shapes=[pltpu.VMEM((CONV_HIST + tc, D_MODEL), F32),
                        pltpu.VMEM((7, CONV_HIST + tc - 8, D_MODEL), F32)],
        compiler_params=_cparams(("parallel", "arbitrary")),
        name="conv_prompt",
    )(z3, wb, conv_b.reshape(1, D_MODEL))


def _conv_sample_kernel(u_ref, w_ref, bias_ref, s_ref, y_ref, so_ref, *, bb):
    hist = CONV_W - 1
    for b in range(bb):
        u = u_ref[b:b + 1, :]
        y = jnp.sum(s_ref[b] * w_ref[0:hist, :], axis=0, keepdims=True)
        y_ref[b:b + 1, :] = y + w_ref[hist:CONV_W, :] * u + bias_ref[...]
        so_ref[b, 0:hist - 1, :] = s_ref[b, 1:hist, :]
        so_ref[b, hist - 1:hist, :] = u


def _conv_sample(z, states, layer, conv_w, conv_b, bb):
    n = z.shape[0]
    state_block = (bb, CONV_W - 1, D_MODEL)
    return pl.pallas_call(
        functools.partial(_conv_sample_kernel, bb=bb),
        grid=(n // bb,),
        in_specs=[
            pl.BlockSpec((bb, D_MODEL), lambda i: (i, 0)),
            pl.BlockSpec((CONV_W, D_MODEL), lambda i: (0, 0)),
            pl.BlockSpec((1, D_MODEL), lambda i: (0, 0)),
            pl.BlockSpec((None,) + state_block, lambda i: (layer, i, 0, 0)),
        ],
        out_specs=[pl.BlockSpec((bb, D_MODEL), lambda i: (i, 0)),
                   pl.BlockSpec(state_block, lambda i: (i, 0, 0))],
        out_shape=[jax.ShapeDtypeStruct((n, D_MODEL), F32),
                   jax.ShapeDtypeStruct(states.shape[1:], F32)],
        compiler_params=_cparams(("parallel",)),
        name="conv_sample",
    )(z, conv_w, conv_b.reshape(1, D_MODEL), states)


def _merge_kernel(yc_ref, oa_ref, ga_ref, gb_ref, x_ref, g1_ref, sh2_ref, sc2_ref,
                  lng_ref, lnb_ref, n2g_ref, wco_ref, whg_ref, wo_ref, x1_ref, h2_ref):
    yc = yc_ref[...]
    xc = yc - jnp.mean(yc, axis=-1, keepdims=True)
    vln = xc * lax.rsqrt(jnp.mean(xc * xc, axis=-1, keepdims=True) + EPS)
    va = _silu(vln * lng_ref[...] + lnb_ref[...]).astype(BF16)
    y_b = jnp.dot(va, wco_ref[...], preferred_element_type=F32)
    y_a = jnp.dot(oa_ref[...], whg_ref[...], preferred_element_type=F32)
    mixed = (ga_ref[...] * y_a + gb_ref[...] * y_b).astype(BF16)
    x1 = x_ref[...] + g1_ref[...] * jnp.dot(mixed, wo_ref[...], preferred_element_type=F32)
    x1_ref[...] = x1
    y = x1 * lax.rsqrt(jnp.mean(x1 * x1, axis=-1, keepdims=True) + EPS) * n2g_ref[...]
    h2_ref[...] = (y * (1.0 + sc2_ref[...]) + sh2_ref[...]).astype(h2_ref.dtype)


def _merge(yc, oa, z, x, mod, ln_g, ln_b, n2_g, wco, whg, wo, layer, tm, rows_per_mod,
           h2_dtype=BF16):
    m = x.shape[0]
    mod_rows = mod.shape[1]

    def mod_spec(col):
        return pl.BlockSpec((None, mod_rows, D_MODEL),
                            lambda i: ((i * tm) // rows_per_mod, 0, col))

    def row_spec(col=0):
        return pl.BlockSpec((tm, D_MODEL), lambda i: (i, col))

    vec_spec = pl.BlockSpec((1, D_MODEL), lambda i: (0, 0))
    w_spec = pl.BlockSpec((None, D_MODEL, D_MODEL), lambda i: (layer, 0, 0))
    return pl.pallas_call(
        _merge_kernel,
        grid=(m // tm,),
        in_specs=[row_spec(), row_spec(), row_spec(4), row_spec(5), row_spec(),
                  mod_spec(2), mod_spec(3), mod_spec(4),
                  vec_spec, vec_spec, vec_spec, w_spec, w_spec, w_spec],
        out_specs=[row_spec(), row_spec()],
        out_shape=[jax.ShapeDtypeStruct((m, D_MODEL), F32),
                   jax.ShapeDtypeStruct((m, D_MODEL), h2_dtype)],
        compiler_params=_cparams(("parallel",)),
        name="merge",
    )(yc, oa, z, z, x, mod, mod, mod,
      ln_g.reshape(1, D_MODEL), ln_b.reshape(1, D_MODEL), n2_g.reshape(1, D_MODEL),
      wco, whg, wo)


def _swiglu_rows(x, wg_ref, wu_ref, wd_ref, tf, between=None):
    acc = None
    for j in range(D_FF // tf):
        if between is not None:
            between(j)
        cols = slice(j * tf, (j + 1) * tf)
        a = _silu(jnp.dot(x, wg_ref[:, cols], preferred_element_type=F32))
        a = a * jnp.dot(x, wu_ref[:, cols], preferred_element_type=F32)
        d = jnp.dot(a.astype(BF16), wd_ref[cols, :], preferred_element_type=F32)
        acc = d if acc is None else acc + d
    return acc


def _ffn_kernel(h_ref, x1_ref, g2_ref, wg_ref, wu_ref, wd_ref, o_ref, *, tf):
    o_ref[...] = x1_ref[...] + g2_ref[...] * _swiglu_rows(h_ref[...], wg_ref, wu_ref, wd_ref, tf)


def _ffn(h2, x1, mod, wg, wu, wd, tm, tf, rows_per_mod):
    m = x1.shape[0]
    mod_rows = mod.shape[1]
    w_up_spec = pl.BlockSpec((None, D_MODEL, D_FF), lambda i: (0, 0, 0), pipeline_mode=pl.Buffered(1))
    return pl.pallas_call(
        functools.partial(_ffn_kernel, tf=tf),
        grid=(m // tm,),
        in_specs=[
            pl.BlockSpec((tm, D_MODEL), lambda i: (i, 0)),
            pl.BlockSpec((tm, D_MODEL), lambda i: (i, 0)),
            pl.BlockSpec((None, mod_rows, D_MODEL), lambda i: ((i * tm) // rows_per_mod, 0, 5)),
            w_up_spec, w_up_spec,
            pl.BlockSpec((None, D_FF, D_MODEL), lambda i: (0, 0, 0), pipeline_mode=pl.Buffered(1)),
        ],
        out_specs=pl.BlockSpec((tm, D_MODEL), lambda i: (i, 0)),
        out_shape=jax.ShapeDtypeStruct((m, D_MODEL), F32),
        compiler_params=_cparams(("parallel",)),
        name="ffn",
    )(h2, x1, mod, wg, wu, wd)


def _top2(logits, lane):
    neg = jnp.float32(-jnp.inf)
    logits = jnp.where(lane < N_EXPERTS, logits, neg)
    v1 = jnp.max(logits, axis=-1, keepdims=True)
    i1 = jnp.min(jnp.where(logits == v1, lane, LANES), axis=-1, keepdims=True)
    rest = jnp.where(lane == i1, neg, logits)
    v2 = jnp.max(rest, axis=-1, keepdims=True)
    i2 = jnp.min(jnp.where(rest == v2, lane, LANES), axis=-1, keepdims=True)
    e2 = jnp.exp(v2 - v1)
    w1 = 1.0 / (1.0 + e2)
    return i1, i2, w1, e2 * w1


def _route_kernel(h_ref, rw_ref, r_ref):
    lane = lax.broadcasted_iota(jnp.int32, r_ref.shape, 1)
    logits = jnp.dot(h_ref[...].astype(BF16), rw_ref[...], preferred_element_type=F32)
    i1, i2, w1, w2 = _top2(logits, lane)
    r_ref[...] = (jnp.where(lane == 0, i1.astype(F32), 0.0) + jnp.where(lane == 1, i2.astype(F32), 0.0)
                  + jnp.where(lane == 2, w1, 0.0) + jnp.where(lane == 3, w2, 0.0))


def _route(h2, router_pad, tm):
    m = h2.shape[0]
    return pl.pallas_call(
        _route_kernel,
        grid=(m // tm,),
        in_specs=[pl.BlockSpec((tm, D_MODEL), lambda i: (i, 0)),
                  pl.BlockSpec((D_MODEL, LANES), lambda i: (0, 0))],
        out_specs=pl.BlockSpec((tm, LANES), lambda i: (i, 0)),
        out_shape=jax.ShapeDtypeStruct((m, LANES), F32),
        compiler_params=_cparams(("parallel",)),
        name="route",
    )(h2, router_pad)


def _start_row_gather(idx_ref, n_rows, src_hbm, dst, sem):
    def body(r, carry):
        pltpu.make_async_copy(src_hbm.at[pl.ds(idx_ref[r], 1), :], dst.at[pl.ds(r, 1), :], sem).start()
        return carry
    lax.fori_loop(0, n_rows, body, 0, unroll=8)


def _moe_dispatch_kernel(lo_ref, hi_ref, pos_ref, h_ref, x_hbm, stage, zero_row, sem, pad_sem):
    i = pl.program_id(0)
    last = pl.num_programs(0) - 1
    slot = lax.rem(i, 2)
    tb = stage.shape[1]

    def wait_block(s):
        for _ in range(2):
            pltpu.make_async_copy(stage.at[s], stage.at[s], sem.at[s]).wait()

    @pl.when(i >= 2)
    def _():
        wait_block(slot)

    stage[slot] = h_ref[...]

    def body(r, carry):
        row = stage.at[slot, pl.ds(r, 1), :]
        pltpu.make_async_copy(row, x_hbm.at[pl.ds(pos_ref[0, r], 1), :], sem.at[slot]).start()
        pltpu.make_async_copy(row, x_hbm.at[pl.ds(pos_ref[0, tb + r], 1), :], sem.at[slot]).start()
        return carry
    lax.fori_loop(0, tb, body, 0, unroll=4)

    @pl.when(i == last)
    def _():
        zero_row[...] = jnp.zeros_like(zero_row)
        pad_copy = lambda r: pltpu.make_async_copy(zero_row.at[pl.ds(0, 1), :],
                                                   x_hbm.at[pl.ds(r, 1), :], pad_sem.at[0])

        def start(r, carry):
            pad_copy(r).start()
            return carry

        def wait(r, carry):
            pad_copy(r).wait()
            return carry

        for e in range(lo_ref.shape[0]):
            lax.fori_loop(lo_ref[e], hi_ref[e], start, 0)
        for e in range(lo_ref.shape[0]):
            lax.fori_loop(lo_ref[e], hi_ref[e], wait, 0)
        wait_block(1 - slot)
        wait_block(slot)


def _moe_dispatch(h2, pos_blocks, pad_lo, pad_hi, n_rows):
    n_blocks = pos_blocks.shape[0]
    tb = pos_blocks.shape[2] // 2
    assert n_blocks >= 2
    grid_spec = pltpu.PrefetchScalarGridSpec(
        num_scalar_prefetch=2,
        grid=(n_blocks,),
        in_specs=[
            pl.BlockSpec((None, 1, 2 * tb), lambda i, lo, hi: (i, 0, 0), memory_space=pltpu.SMEM),
            pl.BlockSpec((tb, D_MODEL), lambda i, lo, hi: (i, 0)),
        ],
        out_specs=pl.BlockSpec(memory_space=pl.ANY),
        scratch_shapes=[pltpu.VMEM((2, tb, D_MODEL), F32), pltpu.VMEM((8, D_MODEL), F32),
                        pltpu.SemaphoreType.DMA((2,)), pltpu.SemaphoreType.DMA((1,))],
    )
    return pl.pallas_call(
        _moe_dispatch_kernel,
        grid_spec=grid_spec,
        out_shape=jax.ShapeDtypeStruct((n_rows, D_MODEL), F32),
        compiler_params=_cparams(("arbitrary",)),
        name="moe_dispatch",
    )(pad_lo, pad_hi, pos_blocks, h2)


def _moe_tiles_kernel(te_ref, na_ref, x_ref, wg_ref, wu_ref, wd_ref, y_ref, *, tf):
    @pl.when(pl.program_id(0) < na_ref[0])
    def _():
        y_ref[...] = _swiglu_rows(x_ref[...].astype(BF16), wg_ref, wu_ref, wd_ref, tf)

    @pl.when(pl.program_id(0) >= na_ref[0])
    def _():
        y_ref[...] = jnp.zeros_like(y_ref)


def _moe_tiles(x_sorted, tile_expert, n_active, wg, wu, wd, tm, tf):
    n_tiles = x_sorted.shape[0] // tm
    w_up_spec = pl.BlockSpec((None, None, D_MODEL, D_FF), lambda i, te, na: (0, te[i], 0, 0))
    grid_spec = pltpu.PrefetchScalarGridSpec(
        num_scalar_prefetch=2,
        grid=(n_tiles,),
        in_specs=[
            pl.BlockSpec((tm, D_MODEL), lambda i, te, na: (jnp.minimum(i, na[0] - 1), 0)),
            w_up_spec, w_up_spec,
            pl.BlockSpec((None, None, D_FF, D_MODEL), lambda i, te, na: (0, te[i], 0, 0)),
        ],
        out_specs=pl.BlockSpec((tm, D_MODEL), lambda i, te, na: (i, 0)),
    )
    return pl.pallas_call(
        functools.partial(_moe_tiles_kernel, tf=tf),
        grid_spec=grid_spec,
        out_shape=jax.ShapeDtypeStruct((n_tiles * tm, D_MODEL), F32),
        compiler_params=pltpu.CompilerParams(dimension_semantics=("arbitrary",),
                                             vmem_limit_bytes=MOE_VMEM_LIMIT),
        name="moe_tiles",
    )(tile_expert, n_active, x_sorted, wg, wu, wd)


def _moe_combine_kernel(pos_ref, pos_next_ref, y_hbm, r_ref, x1_ref, g2_ref, fg_ref, o_ref,
                        buf, sem):
    i = pl.program_id(0)
    slot = lax.rem(i, 2)
    n_rows = buf.shape[1]

    @pl.when(i == 0)
    def _():
        _start_row_gather(pos_ref.at[0], n_rows, y_hbm, buf.at[0], sem.at[0])

    @pl.when(i + 1 < pl.num_programs(0))
    def _():
        _start_row_gather(pos_next_ref.at[0], n_rows, y_hbm, buf.at[1 - slot], sem.at[1 - slot])

    pltpu.make_async_copy(buf.at[slot], buf.at[slot], sem.at[slot]).wait()
    tc = n_rows // 2
    lane = lax.broadcasted_iota(jnp.int32, r_ref.shape, 1)
    r = r_ref[...]
    w1 = jnp.sum(jnp.where(lane == 2, r, 0.0), axis=-1, keepdims=True)
    w2 = jnp.sum(jnp.where(lane == 3, r, 0.0), axis=-1, keepdims=True)
    f = w1 * buf[slot, 0:tc, :] + w2 * buf[slot, tc:n_rows, :]
    x2 = x1_ref[...] + g2_ref[...] * f
    o_ref[...] = x2 * lax.rsqrt(jnp.mean(x2 * x2, axis=-1, keepdims=True) + EPS) * fg_ref[...]


def _moe_combine(y_sorted, pos, route, x1, mod, final_g, tc, rows_per_mod):
    m = x1.shape[0]
    n_blocks = m // tc
    mod_rows = mod.shape[1]
    pos_spec = lambda f: pl.BlockSpec((None, 1, 2 * tc), f, memory_space=pltpu.SMEM)
    return pl.pallas_call(
        _moe_combine_kernel,
        grid=(n_blocks,),
        in_specs=[
            pos_spec(lambda i: (i, 0, 0)),
            pos_spec(lambda i: (jnp.minimum(i + 1, n_blocks - 1), 0, 0)),
            pl.BlockSpec(memory_space=pl.ANY),
            pl.BlockSpec((tc, LANES), lambda i: (i, 0)),
            pl.BlockSpec((tc, D_MODEL), lambda i: (i, 0)),
            pl.BlockSpec((None, mod_rows, D_MODEL), lambda i: ((i * tc) // rows_per_mod, 0, 5)),
            pl.BlockSpec((1, D_MODEL), lambda i: (0, 0)),
        ],
        out_specs=pl.BlockSpec((tc, D_MODEL), lambda i: (i, 0)),
        out_shape=jax.ShapeDtypeStruct((m, D_MODEL), F32),
        scratch_shapes=[pltpu.VMEM((2, 2 * tc, D_MODEL), F32), pltpu.SemaphoreType.DMA((2,))],
        compiler_params=_cparams(("arbitrary",)),
        name="moe_combine",
    )(pos, pos, y_sorted, route, x1, mod, final_g.reshape(1, D_MODEL))


def _moe_routed(h2, x1, mod, router_pad, wg, wu, wd, final_g, tm, tf, tc, rows_per_mod):
    n = h2.shape[0]
    route = _route(h2, router_pad, 512)
    expert = route[:, :2].astype(jnp.int32).reshape(2 * n)
    onehot = (expert[:, None] == jnp.arange(N_EXPERTS, dtype=jnp.int32)[None, :]).astype(jnp.int32)
    running = jnp.cumsum(onehot, axis=0)
    rank = jnp.sum((running - onehot) * onehot, axis=1)
    count = running[-1]
    padded = ((count + tm - 1) // tm) * tm
    group_end = jnp.cumsum(padded)
    group_start = group_end - padded
    pos = group_start[expert] + rank
    n_tiles = (2 * n) // tm + N_EXPERTS
    n_rows = n_tiles * tm
    tile_start = jnp.arange(n_tiles, dtype=jnp.int32) * tm
    tile_expert = jnp.minimum(jnp.sum(tile_start[:, None] >= group_end[None, :], axis=1),
                              N_EXPERTS - 1).astype(jnp.int32)
    n_active = (group_end[-1:] // tm).astype(jnp.int32)
    pad_lo = jnp.concatenate([group_start + count, group_end[-1:]]).astype(jnp.int32)
    pad_hi = jnp.concatenate([group_end, jnp.full((1,), n_rows)]).astype(jnp.int32)
    pos_blocks = pos.reshape(n // tc, tc, 2).transpose(0, 2, 1).reshape(n // tc, 1, 2 * tc)

    x_sorted = _moe_dispatch(h2, pos_blocks, pad_lo, pad_hi, n_rows)
    y_sorted = _moe_tiles(x_sorted, tile_expert, n_active, wg, wu, wd, tm, tf)
    return _moe_combine(y_sorted, pos_blocks, route, x1, mod, final_g, tc, rows_per_mod)


def _moe_kernel(h_ref, x1_ref, g2_ref, rw_ref, wg_ref, wu_ref, wd_ref, fg_ref, o_ref,
                acc, comb):
    e = pl.program_id(1)
    j = pl.program_id(2)
    lane = lax.broadcasted_iota(jnp.int32, comb.shape, 1)

    @pl.when((e == 0) & (j == 0))
    def _():
        acc[...] = jnp.zeros_like(acc)
        i1, i2, w1, w2 = _top2(jnp.dot(h_ref[...], rw_ref[...], preferred_element_type=F32), lane)
        comb[...] = jnp.where(lane == i1, w1, 0.0) + jnp.where(lane == i2, w2, 0.0)

    ce = jnp.sum(jnp.where(lane == e, comb[...], 0.0), axis=-1, keepdims=True)
    h = h_ref[...]
    a = _silu(jnp.dot(h, wg_ref[...], preferred_element_type=F32))
    a = a * jnp.dot(h, wu_ref[...], preferred_element_type=F32) * ce
    acc[...] += jnp.dot(a.astype(BF16), wd_ref[...], preferred_element_type=F32)

    @pl.when((e == pl.num_programs(1) - 1) & (j == pl.num_programs(2) - 1))
    def _():
        x2 = x1_ref[...] + g2_ref[...] * acc[...]
        o_ref[...] = x2 * lax.rsqrt(jnp.mean(x2 * x2, axis=-1, keepdims=True) + EPS) * fg_ref[...]


def _moe(h2, x1, mod, router_pad, wg, wu, wd, final_g, tm, tf, rows_per_mod):
    m = x1.shape[0]
    mod_rows = mod.shape[1]
    return pl.pallas_call(
        _moe_kernel,
        grid=(m // tm, N_EXPERTS, D_FF // tf),
        in_specs=[
            pl.BlockSpec((tm, D_MODEL), lambda i, e, j: (i, 0)),
            pl.BlockSpec((tm, D_MODEL), lambda i, e, j: (i, 0)),
            pl.BlockSpec((None, mod_rows, D_MODEL),
                         lambda i, e, j: ((i * tm) // rows_per_mod, 0, 5)),
            pl.BlockSpec((D_MODEL, LANES), lambda i, e, j: (0, 0)),
            pl.BlockSpec((None, None, D_MODEL, tf), lambda i, e, j: (0, e, 0, j)),
            pl.BlockSpec((None, None, D_MODEL, tf), lambda i, e, j: (0, e, 0, j)),
            pl.BlockSpec((None, None, tf, D_MODEL), lambda i, e, j: (0, e, j, 0)),
            pl.BlockSpec((1, D_MODEL), lambda i, e, j: (0, 0)),
        ],
        out_specs=pl.BlockSpec((tm, D_MODEL), lambda i, e, j: (i, 0)),
        out_shape=jax.ShapeDtypeStruct((m, D_MODEL), F32),
        scratch_shapes=[pltpu.VMEM((tm, D_MODEL), F32), pltpu.VMEM((tm, LANES), F32)],
        compiler_params=_cparams(("parallel", "arbitrary", "arbitrary")),
        name="moe",
    )(h2, x1, mod, router_pad, wg, wu, wd, final_g.reshape(1, D_MODEL))


def kernel(x_prompt, x_sample, state_hgrn, state_conv, c_prompt, c_sample, ada_w, ada_b, norm1_g, norm2_g, w_in, lb_logits, hg_norm_g, w_hg_out, conv_w, conv_b, conv_ln_g, conv_ln_b, w_conv_out, w_o, ffn_w_gate, ffn_w_up, ffn_w_down, router_w, moe_w_gate, moe_w_up, moe_w_down, final_norm_g):
    bsz, seq, _ = x_prompt.shape
    n_s = x_sample.shape[0]
    hist = CONV_W - 1

    w_in_b = w_in.astype(BF16)
    whg_b = w_hg_out.astype(BF16)
    wco_b = w_conv_out.astype(BF16)
    wo_b = w_o.astype(BF16)
    ffn_g, ffn_u, ffn_d = (w.astype(BF16) for w in (ffn_w_gate, ffn_w_up, ffn_w_down))
    moe_g, moe_u, moe_d = (w.astype(BF16) for w in (moe_w_gate, moe_w_up, moe_w_down))
    router_pad = jnp.pad(router_w[0], ((0, 0), (0, LANES - N_EXPERTS))).astype(BF16)
    acat, lvl = _hgrn_constants()

    mod_all = _ada(jnp.concatenate([c_prompt, c_sample], axis=0), ada_w, ada_b)

    xp = x_prompt.reshape(bsz * seq, D_MODEL)
    xs = x_sample.reshape(n_s, D_MODEL)
    tm_p = 256
    hgrn_p, conv_p, hgrn_s, conv_s = [], [], [], []
    for l in range(DEPTH):
        mod_p = mod_all[l, :bsz].reshape(bsz, 1, 6 * D_MODEL)
        mod_s = mod_all[l, bsz:].reshape(1, n_s, 6 * D_MODEL)

        zp, up, lfp = _inproj(xp, mod_p, norm1_g[l], lb_logits, w_in_b, l, tm_p, seq, BF16)
        zs, us, lfs = _inproj(xs, mod_s, norm1_g[l], lb_logits, w_in_b, l, n_s, n_s, F32)

        zp3 = zp.reshape(bsz, seq, N_GROUPS_OUT * D_MODEL)
        up3 = up.reshape(bsz, seq, D_MODEL)
        oa_p, s_p = _hgrn_prompt(zp3, lfp.reshape(bsz, seq, D_MODEL), hg_norm_g[l], acat, lvl, 512, 2)
        oa_s, s_s = _hgrn_sample(zs, lfs, hg_norm_g[l], state_hgrn, l, 8)

        yc_p = _conv_prompt(up3, conv_w[l], conv_b[l], 256)
        yc_s, c_s = _conv_sample(us, state_conv, l, conv_w[l], conv_b[l], 8)
        c_p = up3[:, seq - hist:, :]

        merge_w = (conv_ln_g[l], conv_ln_b[l], norm2_g[l], wco_b, whg_b, wo_b, l)
        dense = l % 2 == 0
        x1p, h2p = _merge(yc_p.reshape(bsz * seq, D_MODEL), oa_p.reshape(bsz * seq, D_MODEL),
                          zp, xp, mod_p, *merge_w, 512, seq, BF16 if dense else F32)
        x1s, h2s = _merge(yc_s, oa_s, zs, xs, mod_s, *merge_w, n_s, n_s)

        if dense:
            xp = _ffn(h2p, x1p, mod_p, ffn_g, ffn_u, ffn_d, 512, 256, seq)
            xs = _ffn(h2s, x1s, mod_s, ffn_g, ffn_u, ffn_d, n_s, 256, n_s)
        else:
            xp = _moe_routed(h2p, x1p, mod_p, router_pad, moe_g, moe_u, moe_d, final_norm_g,
                             512, 256, 256, seq)
            xs = _moe(h2s, x1s, mod_s, router_pad, moe_g, moe_u, moe_d, final_norm_g, n_s, 1408, n_s)

        hgrn_p.append(s_p)
        conv_p.append(c_p)
        hgrn_s.append(s_s)
        conv_s.append(c_s)

    return (xp.reshape(bsz, seq, D_MODEL), xs.reshape(n_s, 1, D_MODEL),
            jnp.stack(hgrn_p), jnp.stack(conv_p), jnp.stack(hgrn_s), jnp.stack(conv_s))
```

```python
import functools

import numpy as np
import jax
import jax.numpy as jnp
from jax import lax
from jax.experimental import pallas as pl
from jax.experimental.pallas import tpu as pltpu

F32 = jnp.float32
BF16 = jnp.bfloat16

D_MODEL = 1024
DEPTH = 2
HEADS = 8
HEAD_DIM = 128
CHUNK = 64
CONV_W = 31
N_EXPERTS = 8
D_FF = 2816
EPS = 1e-6
LOG2_E = 1.4426950408889634
LANES = 128
N_GROUPS_OUT = 6
VMEM_LIMIT = 48 * 1024 * 1024
MOE_VMEM_LIMIT = 56 * 1024 * 1024

LEVEL_WIDTHS = tuple(CHUNK >> (i + 1) for i in range(CHUNK.bit_length() - 1))


def _sigmoid(x):
    return 1.0 / (1.0 + jnp.exp(-x))


def _silu(x):
    return x * _sigmoid(x)


def _cparams(sem):
    return pltpu.CompilerParams(dimension_semantics=sem, vmem_limit_bytes=VMEM_LIMIT)


def _ada_kernel(c_ref, w_ref, b_ref, o_ref):
    a = _silu(c_ref[...]).astype(BF16)
    o_ref[...] = jnp.dot(a, w_ref[...].astype(BF16), preferred_element_type=F32) + b_ref[...]


def _ada(c_all, ada_w, ada_b):
    n = c_all.shape[0]
    tn = 1024
    return pl.pallas_call(
        _ada_kernel,
        grid=(DEPTH, 6 * D_MODEL // tn),
        in_specs=[
            pl.BlockSpec((n, D_MODEL), lambda l, j: (0, 0)),
            pl.BlockSpec((None, D_MODEL, tn), lambda l, j: (l, 0, j)),
            pl.BlockSpec((None, 1, tn), lambda l, j: (l, 0, j)),
        ],
        out_specs=pl.BlockSpec((None, n, tn), lambda l, j: (l, 0, j)),
        out_shape=jax.ShapeDtypeStruct((DEPTH, n, 6 * D_MODEL), F32),
        compiler_params=_cparams(("parallel", "parallel")),
        name="ada",
    )(c_all, ada_w, ada_b.reshape(DEPTH, 1, 6 * D_MODEL))


def _inproj_kernel(x_ref, sh_ref, sc_ref, g_ref, lbl_ref, w_ref, z_ref, u_ref, lf_ref, *, layer):
    x = x_ref[...]
    y = x * lax.rsqrt(jnp.mean(x * x, axis=-1, keepdims=True) + EPS) * g_ref[...]
    h = (y * (1.0 + sc_ref[...]) + sh_ref[...]).astype(BF16)

    def proj(group):
        return jnp.dot(h, w_ref[:, group * D_MODEL:(group + 1) * D_MODEL],
                       preferred_element_type=F32)

    def put(group, value):
        z_ref[:, group * D_MODEL:(group + 1) * D_MODEL] = value.astype(z_ref.dtype)

    put(0, _silu(proj(0)))
    lbl = lbl_ref[...]
    e = jnp.exp(lbl - jnp.max(lbl, axis=0, keepdims=True))
    p = e / jnp.sum(e, axis=0, keepdims=True)
    cum = p[0:1]
    for i in range(1, layer + 1):
        cum = cum + p[i:i + 1]
    lb = cum - p[0:1]
    zf = proj(1)
    put(1, (1.0 - lb) * _sigmoid(-zf))
    lf_ref[...] = jnp.log(lb + (1.0 - lb) * _sigmoid(zf))
    put(2, proj(2))
    put(3, _silu(proj(3)))
    u_ref[...] = proj(4) * _sigmoid(proj(5))
    put(4, _sigmoid(proj(6)))
    put(5, _sigmoid(proj(7)))


def _inproj(x, mod, norm_g, lb_logits, w_in_b, layer, tm, rows_per_mod, z_dtype):
    m = x.shape[0]
    mod_rows = mod.shape[1]
    n_cols = w_in_b.shape[-1]

    def mod_spec(col):
        return pl.BlockSpec((None, mod_rows, D_MODEL),
                            lambda i: ((i * tm) // rows_per_mod, 0, col))

    return pl.pallas_call(
        functools.partial(_inproj_kernel, layer=layer),
        grid=(m // tm,),
        in_specs=[
            pl.BlockSpec((tm, D_MODEL), lambda i: (i, 0)),
            mod_spec(0), mod_spec(1),
            pl.BlockSpec((1, D_MODEL), lambda i: (0, 0)),
            pl.BlockSpec((DEPTH, D_MODEL), lambda i: (0, 0)),
            pl.BlockSpec((None, D_MODEL, n_cols), lambda i: (layer, 0, 0),
                         pipeline_mode=pl.Buffered(1)),
        ],
        out_specs=[
            pl.BlockSpec((tm, N_GROUPS_OUT * D_MODEL), lambda i: (i, 0)),
            pl.BlockSpec((tm, D_MODEL), lambda i: (i, 0)),
            pl.BlockSpec((tm, D_MODEL), lambda i: (i, 0)),
        ],
        out_shape=[
            jax.ShapeDtypeStruct((m, N_GROUPS_OUT * D_MODEL), z_dtype),
            jax.ShapeDtypeStruct((m, D_MODEL), F32),
            jax.ShapeDtypeStruct((m, D_MODEL), F32),
        ],
        compiler_params=_cparams(("parallel",)),
        name="inproj",
    )(x, mod, mod, norm_g.reshape(1, D_MODEL), lb_logits, w_in_b)


def _hgrn_constants():
    t = np.arange(CHUNK)
    s = t[None, :]
    blocks = [(s <= t[:, None])]
    for w in LEVEL_WIDTHS:
        ref_row = ((t & ~(2 * w - 1)) + w - 1)[:, None]
        is_query = ((t & w) != 0)[:, None]
        blocks.append(np.where(is_query, (s > ref_row) & (s <= t[:, None]),
                               (s > t[:, None]) & (s <= ref_row)))
    cum = np.concatenate(blocks, axis=0).astype(np.float32)
    acat = np.concatenate([cum, cum, cum, np.zeros_like(cum)], axis=1)
    x = t[:, None] ^ t[None, :]
    lvl = np.full((CHUNK, CHUNK), -1, np.int32)
    for li, w in enumerate(LEVEL_WIDTHS):
        lvl[((x // w) == 1) & ((t[:, None] & w) != 0)] = li
    lvl[t[:, None] == t[None, :]] = len(LEVEL_WIDTHS)
    return jnp.asarray(acat, BF16), jnp.asarray(np.tile(lvl, (1, 2)))


def _block_diag(a, b):
    z = jnp.zeros_like(a)
    return jnp.concatenate([jnp.concatenate([a, z], axis=1),
                            jnp.concatenate([z, b], axis=1)], axis=0)


def _hgrn_prompt_kernel(q_ref, k_ref, v_ref, og_ref, lf_ref, gn_ref, acat_ref, lvl_ref,
                        o_ref, s_ref, st_scr, ops_scr, dec_scr, *, n_chunks):
    tb = pl.program_id(1)
    seqs = range(st_scr.shape[0])

    @pl.when(tb == 0)
    def _():
        st_scr[...] = jnp.zeros_like(st_scr)

    lvl = lvl_ref[...]
    trans_b = (((1,), (1,)), ((), ()))
    trans_a = (((0,), (0,)), ((), ()))
    pair_w = 2 * HEAD_DIM
    n_pairs = HEADS // 2
    n_lvl = len(LEVEL_WIDTHS)

    def head(x, h):
        return x[:, h * HEAD_DIM:(h + 1) * HEAD_DIM]

    def pair_rows(x, p):
        return _block_diag(head(x, 2 * p), head(x, 2 * p + 1))

    def lanes(p):
        return slice(p * pair_w, (p + 1) * pair_w)

    def chunk_rows(c):
        return pl.ds(pl.multiple_of(c * CHUNK, CHUNK), CHUNK)

    qe_at, kd_at = 2 * n_lvl, 2 * n_lvl + 1

    def prepare(c, slot):
        rows = chunk_rows(c)
        for s in seqs:
            lf = lf_ref[s, rows, :] * LOG2_E
            hi = lf.astype(BF16)
            r1 = lf - hi.astype(F32)
            mid = r1.astype(BF16)
            lo = (r1 - mid.astype(F32)).astype(BF16)
            lf4 = jnp.concatenate([hi, mid, lo, jnp.zeros_like(hi)], axis=0)
            cums = jnp.dot(acat_ref[...], lf4, preferred_element_type=F32)
            qb = q_ref[s, rows, :]
            kb = k_ref[s, rows, :]
            for li in range(n_lvl):
                e = jnp.exp2(cums[(li + 1) * CHUNK:(li + 2) * CHUNK]).astype(BF16)
                ops_scr[slot, s, li] = qb * e
                ops_scr[slot, s, n_lvl + li] = kb * e
            b = cums[0:CHUNK]
            b_last = b[CHUNK - 1:CHUNK]
            ops_scr[slot, s, qe_at] = qb * jnp.exp2(b).astype(BF16)
            ops_scr[slot, s, kd_at] = kb * jnp.exp2(b_last - b).astype(BF16)
            dec_scr[slot, s] = jnp.broadcast_to(jnp.exp2(b_last), dec_scr.shape[2:])

    def consume(c, slot):
        rows = chunk_rows(c)
        qb = [q_ref[s, rows, :] for s in seqs]
        kb = [k_ref[s, rows, :] for s in seqs]
        v = [v_ref[s, rows, :] for s in seqs]
        att = [[jnp.where(lvl == n_lvl,
                          lax.dot_general(qb[s][:, lanes(p)], pair_rows(kb[s], p), trans_b,
                                          preferred_element_type=F32), 0.0)
                for p in range(n_pairs)] for s in seqs]
        for li in range(n_lvl):
            for s in seqs:
                xq = ops_scr[slot, s, li]
                xk = ops_scr[slot, s, n_lvl + li]
                for p in range(n_pairs):
                    g = lax.dot_general(xq[:, lanes(p)], pair_rows(xk, p), trans_b,
                                        preferred_element_type=F32)
                    att[s][p] = jnp.where(lvl == li, g, att[s][p])
        inter = []
        for s in seqs:
            qe = ops_scr[slot, s, qe_at]
            inter.append([lax.dot_general(
                qe[:, lanes(p)],
                _block_diag(st_scr[s, 2 * p].astype(BF16), st_scr[s, 2 * p + 1].astype(BF16)),
                trans_b, preferred_element_type=F32) for p in range(n_pairs)])
        for s in seqs:
            kd = ops_scr[slot, s, kd_at]
            decay = dec_scr[slot, s, 0:1, :]
            for h in range(HEADS):
                st_scr[s, h] = head(decay, h) * st_scr[s, h] + lax.dot_general(
                    head(v[s], h), head(kd, h), trans_a, preferred_element_type=F32)
        for s in seqs:
            for p in range(n_pairs):
                o_pair = inter[s][p] + jnp.dot(att[s][p].astype(BF16), pair_rows(v[s], p),
                                               preferred_element_type=F32)
                for h in (2 * p, 2 * p + 1):
                    sl = slice(h * HEAD_DIM, (h + 1) * HEAD_DIM)
                    o = head(o_pair, h % 2)
                    o = o * lax.rsqrt(jnp.mean(o * o, axis=-1, keepdims=True) + EPS)
                    o = o * gn_ref[:, sl] * og_ref[s, rows, sl].astype(F32)
                    o_ref[s, rows, sl] = o.astype(BF16)

    assert n_chunks % 2 == 0
    prepare(0, 0)

    def chunk_pair(j, carry):
        prepare(2 * j + 1, 1)
        consume(2 * j, 0)
        prepare(2 * j + 2, 0)
        consume(2 * j + 1, 1)
        return carry

    lax.fori_loop(0, n_chunks // 2 - 1, chunk_pair, 0)
    prepare(n_chunks - 1, 1)
    consume(n_chunks - 2, 0)
    consume(n_chunks - 1, 1)

    @pl.when(tb == pl.num_programs(1) - 1)
    def _():
        for s in seqs:
            for h in range(HEADS):
                s_ref[s, h] = st_scr[s, h].T


def _hgrn_prompt(z3, lf3, gn, acat, lvl, tb, nb):
    bsz, t, _ = z3.shape

    def zspec(col):
        return pl.BlockSpec((nb, tb, D_MODEL), lambda b, i: (b, i, col))

    return pl.pallas_call(
        functools.partial(_hgrn_prompt_kernel, n_chunks=tb // CHUNK),
        grid=(bsz // nb, t // tb),
        in_specs=[
            zspec(0), zspec(1), zspec(2), zspec(3),
            pl.BlockSpec((nb, tb, D_MODEL), lambda b, i: (b, i, 0)),
            pl.BlockSpec((1, D_MODEL), lambda b, i: (0, 0)),
            pl.BlockSpec(acat.shape, lambda b, i: (0, 0)),
            pl.BlockSpec(lvl.shape, lambda b, i: (0, 0)),
        ],
        out_specs=[
            pl.BlockSpec((nb, tb, D_MODEL), lambda b, i: (b, i, 0)),
            pl.BlockSpec((nb, HEADS, HEAD_DIM, HEAD_DIM), lambda b, i: (b, 0, 0, 0)),
        ],
        out_shape=[
            jax.ShapeDtypeStruct((bsz, t, D_MODEL), BF16),
            jax.ShapeDtypeStruct((bsz, HEADS, HEAD_DIM, HEAD_DIM), F32),
        ],
        scratch_shapes=[
            pltpu.VMEM((nb, HEADS, HEAD_DIM, HEAD_DIM), F32),
            pltpu.VMEM((2, nb, 2 * len(LEVEL_WIDTHS) + 2, CHUNK, D_MODEL), BF16),
            pltpu.VMEM((2, nb, 8, D_MODEL), F32),
        ],
        compiler_params=_cparams(("parallel", "arbitrary")),
        name="hgrn_prompt",
    )(z3, z3, z3, z3, lf3, gn.reshape(1, D_MODEL), acat, lvl)


def _hgrn_sample_kernel(q_ref, k_ref, v_ref, og_ref, lf_ref, gn_ref, s_ref, *rest, bb):
    o_ref, so_ref, o_scr = rest[-3:]
    n = bb * HEADS

    @pl.when(pl.program_id(0) == 0)
    def _():
        qt = q_ref[...].reshape(n, HEAD_DIM).T
        kt = k_ref[...].reshape(n, HEAD_DIM).T
        ft = jnp.exp(lf_ref[...].reshape(n, HEAD_DIM)).T
        for b in range(bb):
            for h in range(HEADS):
                c = b * HEADS + h
                s_new = ft[:, c:c + 1] * s_ref[b, h] + kt[:, c:c + 1] * v_ref[b, h:h + 1, :]
                so_ref[b, h] = s_new
                o_scr[c:c + 1, :] = jnp.sum(qt[:, c:c + 1] * s_new, axis=0, keepdims=True)
        o = o_scr[...]
        o = o * lax.rsqrt(jnp.mean(o * o, axis=-1, keepdims=True) + EPS)
        o = o * gn_ref[...] * og_ref[...].reshape(n, HEAD_DIM)
        o_ref[...] = o.astype(BF16)

    @pl.when(pl.program_id(0) > 0)
    def _():
        so_ref[...] = jnp.zeros_like(so_ref)


def _stacked_state_io(states, layer, stacked, block):
    n_blocks = states.shape[1] // block[0]
    zeros = (0,) * (len(block) - 1)
    blk = lambda s, i: jnp.where(s == 0, i, n_blocks - 1)
    in_specs = [pl.BlockSpec((None,) + block, lambda s, i: (layer, blk(s, i)) + zeros)]
    operands = [states]
    out_spec = pl.BlockSpec((None,) + block, lambda s, i: (layer + s, i) + zeros)
    if stacked is None:
        return states.shape[0] - layer, blk, in_specs, operands, out_spec
    in_specs.append(pl.BlockSpec(memory_space=pl.ANY))
    operands.append(stacked)
    return 1, blk, in_specs, operands, out_spec


def _hgrn_sample(z, lf, gn, states, layer, stacked, bb):
    n = z.shape[0]
    z3 = z.reshape(n, N_GROUPS_OUT * HEADS, HEAD_DIM)
    lf3 = lf.reshape(n, HEADS, HEAD_DIM)
    gn_rows = jnp.tile(gn.reshape(HEADS, HEAD_DIM), (bb, 1))
    rows, blk, state_in_specs, state_operands, state_out_spec = _stacked_state_io(
        states, layer, stacked, (bb, HEADS, HEAD_DIM, HEAD_DIM))

    def zspec(col):
        return pl.BlockSpec((bb, HEADS, HEAD_DIM), lambda s, i: (blk(s, i), col, 0))

    n_in = 6 + len(state_operands)
    o, s_new = pl.pallas_call(
        functools.partial(_hgrn_sample_kernel, bb=bb),
        grid=(rows, n // bb),
        in_specs=[
            zspec(0), zspec(1), zspec(2), zspec(3),
            pl.BlockSpec((bb, HEADS, HEAD_DIM), lambda s, i: (blk(s, i), 0, 0)),
            pl.BlockSpec((bb * HEADS, HEAD_DIM), lambda s, i: (0, 0)),
        ] + state_in_specs,
        out_specs=[
            pl.BlockSpec((bb * HEADS, HEAD_DIM), lambda s, i: (blk(s, i), 0)),
            state_out_spec,
        ],
        out_shape=[
            jax.ShapeDtypeStruct((n * HEADS, HEAD_DIM), BF16),
            jax.ShapeDtypeStruct(states.shape, F32),
        ],
        scratch_shapes=[pltpu.VMEM((bb * HEADS, HEAD_DIM), F32)],
        input_output_aliases={} if stacked is None else {n_in - 1: 1},
        compiler_params=_cparams(("arbitrary", "arbitrary")),
        name="hgrn_sample",
    )(z3, z3, z3, z3, lf3, gn_rows, *state_operands)
    return o.reshape(n, D_MODEL), s_new


CONV_HIST = 32
CONV_ROWS = 32


def _conv_prompt_kernel(u_ref, wb_ref, bias_ref, y_ref, buf, shifted, *, tc):
    @pl.when(pl.program_id(1) == 0)
    def _():
        buf[0:CONV_HIST, :] = jnp.zeros((CONV_HIST, D_MODEL), F32)

    buf[CONV_HIST:CONV_HIST + tc, :] = u_ref[...]
    lead = CONV_HIST - (CONV_W - 1)
    n_shift = shifted.shape[1]
    for s in range(1, 8):
        shifted[s - 1] = buf[s:s + n_shift, :]

    def rows_body(r, carry):
        base = pl.multiple_of(r * CONV_ROWS, CONV_ROWS)
        accs = [jnp.broadcast_to(bias_ref[...], (8, D_MODEL)) for _ in range(CONV_ROWS // 8)]
        for j in range(CONV_W):
            wj = wb_ref[j * 8:(j + 1) * 8, :]
            p, s = divmod(lead + j, 8)
            for a in range(CONV_ROWS // 8):
                rows = pl.ds(base + 8 * (a + p), 8)
                tap = buf[rows, :] if s == 0 else shifted[s - 1, rows, :]
                accs[a] = accs[a] + wj * tap
        for a in range(CONV_ROWS // 8):
            y_ref[pl.ds(base + a * 8, 8), :] = accs[a]
        return carry

    lax.fori_loop(0, tc // CONV_ROWS, rows_body, 0)
    buf[0:CONV_HIST, :] = buf[tc:tc + CONV_HIST, :]


def _conv_prompt(z3, conv_w, conv_b, tc):
    bsz, t, _ = z3.shape
    wb = jnp.repeat(conv_w, 8, axis=0)
    return pl.pallas_call(
        functools.partial(_conv_prompt_kernel, tc=tc),
        grid=(bsz, t // tc),
        in_specs=[
            pl.BlockSpec((None, tc, D_MODEL), lambda b, i: (b, i, 0)),
            pl.BlockSpec((CONV_W * 8, D_MODEL), lambda b, i: (0, 0)),
            pl.BlockSpec((1, D_MODEL), lambda b, i: (0, 0)),
        ],
        out_specs=pl.BlockSpec((None, tc, D_MODEL), lambda b, i: (b, i, 0)),
        out_shape=jax.ShapeDtypeStruct((bsz, t, D_MODEL), F32),
        scratch_shapes=[pltpu.VMEM((CONV_HIST + tc, D_MODEL), F32),
                        pltpu.VMEM((7, CONV_HIST + tc - 8, D_MODEL), F32)],
        compiler_params=_cparams(("parallel", "arbitrary")),
        name="conv_prompt",
    )(z3, wb, conv_b.reshape(1, D_MODEL))


def _conv_sample_kernel(u_ref, w_ref, bias_ref, s_ref, *rest, bb):
    y_ref, so_ref = rest[-2:]
    hist = CONV_W - 1

    @pl.when(pl.program_id(0) == 0)
    def _():
        for b in range(bb):
            u = u_ref[b:b + 1, :]
            y = jnp.sum(s_ref[b] * w_ref[0:hist, :], axis=0, keepdims=True)
            y_ref[b:b + 1, :] = y + w_ref[hist:CONV_W, :] * u + bias_ref[...]
            so_ref[b, 0:hist - 1, :] = s_ref[b, 1:hist, :]
            so_ref[b, hist - 1:hist, :] = u

    @pl.when(pl.program_id(0) > 0)
    def _():
        so_ref[...] = jnp.zeros_like(so_ref)


def _conv_sample(z, states, layer, stacked, conv_w, conv_b, bb):
    n = z.shape[0]
    rows, blk, state_in_specs, state_operands, state_out_spec = _stacked_state_io(
        states, layer, stacked, (bb, CONV_W - 1, D_MODEL))
    n_in = 3 + len(state_operands)
    return pl.pallas_call(
        functools.partial(_conv_sample_kernel, bb=bb),
        grid=(rows, n // bb),
        in_specs=[
            pl.BlockSpec((bb, D_MODEL), lambda s, i: (blk(s, i), 0)),
            pl.BlockSpec((CONV_W, D_MODEL), lambda s, i: (0, 0)),
            pl.BlockSpec((1, D_MODEL), lambda s, i: (0, 0)),
        ] + state_in_specs,
        out_specs=[pl.BlockSpec((bb, D_MODEL), lambda s, i: (blk(s, i), 0)), state_out_spec],
        out_shape=[jax.ShapeDtypeStruct((n, D_MODEL), F32),
                   jax.ShapeDtypeStruct(states.shape, F32)],
        input_output_aliases={} if stacked is None else {n_in - 1: 1},
        compiler_params=_cparams(("arbitrary", "arbitrary")),
        name="conv_sample",
    )(z, conv_w, conv_b.reshape(1, D_MODEL), *state_operands)


def _merge_kernel(yc_ref, oa_ref, ga_ref, gb_ref, x_ref, g1_ref, sh2_ref, sc2_ref,
                  lng_ref, lnb_ref, n2g_ref, wco_ref, whg_ref, wo_ref, x1_ref, h2_ref):
    yc = yc_ref[...]
    xc = yc - jnp.mean(yc, axis=-1, keepdims=True)
    vln = xc * lax.rsqrt(jnp.mean(xc * xc, axis=-1, keepdims=True) + EPS)
    va = _silu(vln * lng_ref[...] + lnb_ref[...]).astype(BF16)
    y_b = jnp.dot(va, wco_ref[...], preferred_element_type=F32)
    y_a = jnp.dot(oa_ref[...], whg_ref[...], preferred_element_type=F32)
    mixed = (ga_ref[...] * y_a + gb_ref[...] * y_b).astype(BF16)
    x1 = x_ref[...] + g1_ref[...] * jnp.dot(mixed, wo_ref[...], preferred_element_type=F32)
    x1_ref[...] = x1
    y = x1 * lax.rsqrt(jnp.mean(x1 * x1, axis=-1, keepdims=True) + EPS) * n2g_ref[...]
    h2_ref[...] = (y * (1.0 + sc2_ref[...]) + sh2_ref[...]).astype(h2_ref.dtype)


def _merge(yc, oa, z, x, mod, ln_g, ln_b, n2_g, wco, whg, wo, layer, tm, rows_per_mod,
           h2_dtype=BF16):
    m = x.shape[0]
    mod_rows = mod.shape[1]

    def mod_spec(col):
        return pl.BlockSpec((None, mod_rows, D_MODEL),
                            lambda i: ((i * tm) // rows_per_mod, 0, col))

    def row_spec(col=0):
        return pl.BlockSpec((tm, D_MODEL), lambda i: (i, col))

    vec_spec = pl.BlockSpec((1, D_MODEL), lambda i: (0, 0))
    w_spec = pl.BlockSpec((None, D_MODEL, D_MODEL), lambda i: (layer, 0, 0))
    return pl.pallas_call(
        _merge_kernel,
        grid=(m // tm,),
        in_specs=[row_spec(), row_spec(), row_spec(4), row_spec(5), row_spec(),
                  mod_spec(2), mod_spec(3), mod_spec(4),
                  vec_spec, vec_spec, vec_spec, w_spec, w_spec, w_spec],
        out_specs=[row_spec(), row_spec()],
        out_shape=[jax.ShapeDtypeStruct((m, D_MODEL), F32),
                   jax.ShapeDtypeStruct((m, D_MODEL), h2_dtype)],
        compiler_params=_cparams(("parallel",)),
        name="merge",
    )(yc, oa, z, z, x, mod, mod, mod,
      ln_g.reshape(1, D_MODEL), ln_b.reshape(1, D_MODEL), n2_g.reshape(1, D_MODEL),
      wco, whg, wo)


def _swiglu_rows(x, wg_ref, wu_ref, wd_ref, tf, between=None):
    acc = None
    for j in range(D_FF // tf):
        if between is not None:
            between(j)
        cols = slice(j * tf, (j + 1) * tf)
        a = _silu(jnp.dot(x, wg_ref[:, cols], preferred_element_type=F32))
        a = a * jnp.dot(x, wu_ref[:, cols], preferred_element_type=F32)
        d = jnp.dot(a.astype(BF16), wd_ref[cols, :], preferred_element_type=F32)
        acc = d if acc is None else acc + d
    return acc


def _ffn_kernel(h_ref, x1_ref, g2_ref, wg_ref, wu_ref, wd_ref, o_ref, *, tf):
    o_ref[...] = x1_ref[...] + g2_ref[...] * _swiglu_rows(h_ref[...], wg_ref, wu_ref, wd_ref, tf)


def _ffn(h2, x1, mod, wg, wu, wd, tm, tf, rows_per_mod):
    m = x1.shape[0]
    mod_rows = mod.shape[1]
    w_up_spec = pl.BlockSpec((None, D_MODEL, D_FF), lambda i: (0, 0, 0), pipeline_mode=pl.Buffered(1))
    return pl.pallas_call(
        functools.partial(_ffn_kernel, tf=tf),
        grid=(m // tm,),
        in_specs=[
            pl.BlockSpec((tm, D_MODEL), lambda i: (i, 0)),
            pl.BlockSpec((tm, D_MODEL), lambda i: (i, 0)),
            pl.BlockSpec((None, mod_rows, D_MODEL), lambda i: ((i * tm) // rows_per_mod, 0, 5)),
            w_up_spec, w_up_spec,
            pl.BlockSpec((None, D_FF, D_MODEL), lambda i: (0, 0, 0), pipeline_mode=pl.Buffered(1)),
        ],
        out_specs=pl.BlockSpec((tm, D_MODEL), lambda i: (i, 0)),
        out_shape=jax.ShapeDtypeStruct((m, D_MODEL), F32),
        compiler_params=_cparams(("parallel",)),
        name="ffn",
    )(h2, x1, mod, wg, wu, wd)


def _top2(logits, lane):
    neg = jnp.float32(-jnp.inf)
    logits = jnp.where(lane < N_EXPERTS, logits, neg)
    v1 = jnp.max(logits, axis=-1, keepdims=True)
    i1 = jnp.min(jnp.where(logits == v1, lane, LANES), axis=-1, keepdims=True)
    rest = jnp.where(lane == i1, neg, logits)
    v2 = jnp.max(rest, axis=-1, keepdims=True)
    i2 = jnp.min(jnp.where(rest == v2, lane, LANES), axis=-1, keepdims=True)
    e2 = jnp.exp(v2 - v1)
    w1 = 1.0 / (1.0 + e2)
    return i1, i2, w1, e2 * w1


def _route_kernel(h_ref, rw_ref, r_ref):
    lane = lax.broadcasted_iota(jnp.int32, r_ref.shape, 1)
    logits = jnp.dot(h_ref[...].astype(BF16), rw_ref[...], preferred_element_type=F32)
    i1, i2, w1, w2 = _top2(logits, lane)
    r_ref[...] = (jnp.where(lane == 0, i1.astype(F32), 0.0) + jnp.where(lane == 1, i2.astype(F32), 0.0)
                  + jnp.where(lane == 2, w1, 0.0) + jnp.where(lane == 3, w2, 0.0))


def _route(h2, router_pad, tm):
    m = h2.shape[0]
    return pl.pallas_call(
        _route_kernel,
        grid=(m // tm,),
        in_specs=[pl.BlockSpec((tm, D_MODEL), lambda i: (i, 0)),
                  pl.BlockSpec((D_MODEL, LANES), lambda i: (0, 0))],
        out_specs=pl.BlockSpec((tm, LANES), lambda i: (i, 0)),
        out_shape=jax.ShapeDtypeStruct((m, LANES), F32),
        compiler_params=_cparams(("parallel",)),
        name="route",
    )(h2, router_pad)


def _start_row_gather(idx_ref, n_rows, src_hbm, dst, sem):
    for r in range(n_rows):
        pltpu.make_async_copy(src_hbm.at[pl.ds(idx_ref[r], 1), :], dst.at[pl.ds(r, 1), :], sem).start()


def _moe_dispatch_kernel(lo_ref, hi_ref, pos_ref, h_ref, x_hbm, stage, zero_row, sem, pad_sem):
    i = pl.program_id(0)
    last = pl.num_programs(0) - 1
    slot = lax.rem(i, 2)
    tb = stage.shape[1]

    def wait_block(s):
        for _ in range(2):
            pltpu.make_async_copy(stage.at[s], stage.at[s], sem.at[s]).wait()

    @pl.when(i >= 2)
    def _():
        wait_block(slot)

    stage[slot] = h_ref[...]

    for r in range(tb):
        row = stage.at[slot, pl.ds(r, 1), :]
        pltpu.make_async_copy(row, x_hbm.at[pl.ds(pos_ref[0, r], 1), :], sem.at[slot]).start()
        pltpu.make_async_copy(row, x_hbm.at[pl.ds(pos_ref[0, tb + r], 1), :], sem.at[slot]).start()

    @pl.when(i == last)
    def _():
        zero_row[...] = jnp.zeros_like(zero_row)
        pad_copy = lambda r: pltpu.make_async_copy(zero_row.at[pl.ds(0, 1), :],
                                                   x_hbm.at[pl.ds(r, 1), :], pad_sem.at[0])

        def start(r, carry):
            pad_copy(r).start()
            return carry

        def wait(r, carry):
            pad_copy(r).wait()
            return carry

        for e in range(lo_ref.shape[0]):
            lax.fori_loop(lo_ref[e], hi_ref[e], start, 0)
        for e in range(lo_ref.shape[0]):
            lax.fori_loop(lo_ref[e], hi_ref[e], wait, 0)
        wait_block(1 - slot)
        wait_block(slot)


def _moe_dispatch(h2, pos_blocks, pad_lo, pad_hi, n_rows):
    n_blocks = pos_blocks.shape[0]
    tb = pos_blocks.shape[2] // 2
    assert n_blocks >= 2
    grid_spec = pltpu.PrefetchScalarGridSpec(
        num_scalar_prefetch=2,
        grid=(n_blocks,),
        in_specs=[
            pl.BlockSpec((None, 1, 2 * tb), lambda i, lo, hi: (i, 0, 0), memory_space=pltpu.SMEM),
            pl.BlockSpec((tb, D_MODEL), lambda i, lo, hi: (i, 0)),
        ],
        out_specs=pl.BlockSpec(memory_space=pl.ANY),
        scratch_shapes=[pltpu.VMEM((2, tb, D_MODEL), F32), pltpu.VMEM((8, D_MODEL), F32),
                        pltpu.SemaphoreType.DMA((2,)), pltpu.SemaphoreType.DMA((1,))],
    )
    return pl.pallas_call(
        _moe_dispatch_kernel,
        grid_spec=grid_spec,
        out_shape=jax.ShapeDtypeStruct((n_rows, D_MODEL), F32),
        compiler_params=_cparams(("arbitrary",)),
        name="moe_dispatch",
    )(pad_lo, pad_hi, pos_blocks, h2)


def _moe_tiles_kernel(te_ref, na_ref, x_ref, wg_ref, wu_ref, wd_ref, y_ref, *, tf):
    @pl.when(pl.program_id(0) < na_ref[0])
    def _():
        y_ref[...] = _swiglu_rows(x_ref[...].astype(BF16), wg_ref, wu_ref, wd_ref, tf)

    @pl.when(pl.program_id(0) >= na_ref[0])
    def _():
        y_ref[...] = jnp.zeros_like(y_ref)


def _moe_tiles(x_sorted, tile_expert, n_active, wg, wu, wd, tm, tf):
    n_tiles = x_sorted.shape[0] // tm
    w_up_spec = pl.BlockSpec((None, None, D_MODEL, D_FF), lambda i, te, na: (0, te[i], 0, 0))
    grid_spec = pltpu.PrefetchScalarGridSpec(
        num_scalar_prefetch=2,
        grid=(n_tiles,),
        in_specs=[
            pl.BlockSpec((tm, D_MODEL), lambda i, te, na: (jnp.minimum(i, na[0] - 1), 0)),
            w_up_spec, w_up_spec,
            pl.BlockSpec((None, None, D_FF, D_MODEL), lambda i, te, na: (0, te[i], 0, 0)),
        ],
        out_specs=pl.BlockSpec((tm, D_MODEL), lambda i, te, na: (i, 0)),
    )
    return pl.pallas_call(
        functools.partial(_moe_tiles_kernel, tf=tf),
        grid_spec=grid_spec,
        out_shape=jax.ShapeDtypeStruct((n_tiles * tm, D_MODEL), F32),
        compiler_params=pltpu.CompilerParams(dimension_semantics=("arbitrary",),
                                             vmem_limit_bytes=MOE_VMEM_LIMIT),
        name="moe_tiles",
    )(tile_expert, n_active, x_sorted, wg, wu, wd)


def _moe_combine_kernel(pos_ref, pos_next_ref, y_hbm, r_ref, x1_ref, g2_ref, fg_ref, o_ref,
                        buf, sem):
    i = pl.program_id(0)
    slot = lax.rem(i, 2)
    n_rows = buf.shape[1]

    @pl.when(i == 0)
    def _():
        _start_row_gather(pos_ref.at[0], n_rows, y_hbm, buf.at[0], sem.at[0])

    @pl.when(i + 1 < pl.num_programs(0))
    def _():
        _start_row_gather(pos_next_ref.at[0], n_rows, y_hbm, buf.at[1 - slot], sem.at[1 - slot])

    pltpu.make_async_copy(buf.at[slot], buf.at[slot], sem.at[slot]).wait()
    tc = n_rows // 2
    lane = lax.broadcasted_iota(jnp.int32, r_ref.shape, 1)
    r = r_ref[...]
    w1 = jnp.sum(jnp.where(lane == 2, r, 0.0), axis=-1, keepdims=True)
    w2 = jnp.sum(jnp.where(lane == 3, r, 0.0), axis=-1, keepdims=True)
    f = w1 * buf[slot, 0:tc, :] + w2 * buf[slot, tc:n_rows, :]
    x2 = x1_ref[...] + g2_ref[...] * f
    o_ref[...] = x2 * lax.rsqrt(jnp.mean(x2 * x2, axis=-1, keepdims=True) + EPS) * fg_ref[...]


def _moe_combine(y_sorted, pos, route, x1, mod, final_g, tc, rows_per_mod):
    m = x1.shape[0]
    n_blocks = m // tc
    mod_rows = mod.shape[1]
    pos_spec = lambda f: pl.BlockSpec((None, 1, 2 * tc), f, memory_space=pltpu.SMEM)
    return pl.pallas_call(
        _moe_combine_kernel,
        grid=(n_blocks,),
        in_specs=[
            pos_spec(lambda i: (i, 0, 0)),
            pos_spec(lambda i: (jnp.minimum(i + 1, n_blocks - 1), 0, 0)),
            pl.BlockSpec(memory_space=pl.ANY),
            pl.BlockSpec((tc, LANES), lambda i: (i, 0)),
            pl.BlockSpec((tc, D_MODEL), lambda i: (i, 0)),
            pl.BlockSpec((None, mod_rows, D_MODEL), lambda i: ((i * tc) // rows_per_mod, 0, 5)),
            pl.BlockSpec((1, D_MODEL), lambda i: (0, 0)),
        ],
        out_specs=pl.BlockSpec((tc, D_MODEL), lambda i: (i, 0)),
        out_shape=jax.ShapeDtypeStruct((m, D_MODEL), F32),
        scratch_shapes=[pltpu.VMEM((2, 2 * tc, D_MODEL), F32), pltpu.SemaphoreType.DMA((2,))],
        compiler_params=_cparams(("arbitrary",)),
        name="moe_combine",
    )(pos, pos, y_sorted, route, x1, mod, final_g.reshape(1, D_MODEL))


def _moe_routed(h2, x1, mod, router_pad, wg, wu, wd, final_g, tm, tf, tc, rows_per_mod):
    n = h2.shape[0]
    route = _route(h2, router_pad, 512)
    expert = route[:, :2].astype(jnp.int32).reshape(2 * n)
    onehot = (expert[:, None] == jnp.arange(N_EXPERTS, dtype=jnp.int32)[None, :]).astype(jnp.int32)
    running = jnp.cumsum(onehot, axis=0)
    rank = jnp.sum((running - onehot) * onehot, axis=1)
    count = running[-1]
    padded = ((count + tm - 1) // tm) * tm
    group_end = jnp.cumsum(padded)
    group_start = group_end - padded
    pos = group_start[expert] + rank
    n_tiles = (2 * n) // tm + N_EXPERTS
    n_rows = n_tiles * tm
    tile_start = jnp.arange(n_tiles, dtype=jnp.int32) * tm
    tile_expert = jnp.minimum(jnp.sum(tile_start[:, None] >= group_end[None, :], axis=1),
                              N_EXPERTS - 1).astype(jnp.int32)
    n_active = (group_end[-1:] // tm).astype(jnp.int32)
    pad_lo = jnp.concatenate([group_start + count, group_end[-1:]]).astype(jnp.int32)
    pad_hi = jnp.concatenate([group_end, jnp.full((1,), n_rows)]).astype(jnp.int32)
    pos_blocks = pos.reshape(n // tc, tc, 2).transpose(0, 2, 1).reshape(n // tc, 1, 2 * tc)

    x_sorted = _moe_dispatch(h2, pos_blocks, pad_lo, pad_hi, n_rows)
    y_sorted = _moe_tiles(x_sorted, tile_expert, n_active, wg, wu, wd, tm, tf)
    return _moe_combine(y_sorted, pos_blocks, route, x1, mod, final_g, tc, rows_per_mod)


def _moe_kernel(h_ref, x1_ref, g2_ref, rw_ref, wg_ref, wu_ref, wd_ref, fg_ref, o_ref,
                acc, comb):
    e = pl.program_id(1)
    j = pl.program_id(2)
    lane = lax.broadcasted_iota(jnp.int32, comb.shape, 1)

    @pl.when((e == 0) & (j == 0))
    def _():
        acc[...] = jnp.zeros_like(acc)
        i1, i2, w1, w2 = _top2(jnp.dot(h_ref[...], rw_ref[...], preferred_element_type=F32), lane)
        comb[...] = jnp.where(lane == i1, w1, 0.0) + jnp.where(lane == i2, w2, 0.0)

    ce = jnp.sum(jnp.where(lane == e, comb[...], 0.0), axis=-1, keepdims=True)
    h = h_ref[...]
    a = _silu(jnp.dot(h, wg_ref[...], preferred_element_type=F32))
    a = a * jnp.dot(h, wu_ref[...], preferred_element_type=F32) * ce
    acc[...] += jnp.dot(a.astype(BF16), wd_ref[...], preferred_element_type=F32)

    @pl.when((e == pl.num_programs(1) - 1) & (j == pl.num_programs(2) - 1))
    def _():
        x2 = x1_ref[...] + g2_ref[...] * acc[...]
        o_ref[...] = x2 * lax.rsqrt(jnp.mean(x2 * x2, axis=-1, keepdims=True) + EPS) * fg_ref[...]


def _moe(h2, x1, mod, router_pad, wg, wu, wd, final_g, tm, tf, rows_per_mod):
    m = x1.shape[0]
    mod_rows = mod.shape[1]
    return pl.pallas_call(
        _moe_kernel,
        grid=(m // tm, N_EXPERTS, D_FF // tf),
        in_specs=[
            pl.BlockSpec((tm, D_MODEL), lambda i, e, j: (i, 0)),
            pl.BlockSpec((tm, D_MODEL), lambda i, e, j: (i, 0)),
            pl.BlockSpec((None, mod_rows, D_MODEL),
                         lambda i, e, j: ((i * tm) // rows_per_mod, 0, 5)),
            pl.BlockSpec((D_MODEL, LANES), lambda i, e, j: (0, 0)),
            pl.BlockSpec((None, None, D_MODEL, tf), lambda i, e, j: (0, e, 0, j)),
            pl.BlockSpec((None, None, D_MODEL, tf), lambda i, e, j: (0, e, 0, j)),
            pl.BlockSpec((None, None, tf, D_MODEL), lambda i, e, j: (0, e, j, 0)),
            pl.BlockSpec((1, D_MODEL), lambda i, e, j: (0, 0)),
        ],
        out_specs=pl.BlockSpec((tm, D_MODEL), lambda i, e, j: (i, 0)),
        out_shape=jax.ShapeDtypeStruct((m, D_MODEL), F32),
        scratch_shapes=[pltpu.VMEM((tm, D_MODEL), F32), pltpu.VMEM((tm, LANES), F32)],
        compiler_params=_cparams(("parallel", "arbitrary", "arbitrary")),
        name="moe",
    )(h2, x1, mod, router_pad, wg, wu, wd, final_g.reshape(1, D_MODEL))


def kernel(x_prompt, x_sample, state_hgrn, state_conv, c_prompt, c_sample, ada_w, ada_b, norm1_g, norm2_g, w_in, lb_logits, hg_norm_g, w_hg_out, conv_w, conv_b, conv_ln_g, conv_ln_b, w_conv_out, w_o, ffn_w_gate, ffn_w_up, ffn_w_down, router_w, moe_w_gate, moe_w_up, moe_w_down, final_norm_g):
    bsz, seq, _ = x_prompt.shape
    n_s = x_sample.shape[0]
    hist = CONV_W - 1

    w_in_b = w_in.astype(BF16)
    whg_b = w_hg_out.astype(BF16)
    wco_b = w_conv_out.astype(BF16)
    wo_b = w_o.astype(BF16)
    ffn_g, ffn_u, ffn_d = (w.astype(BF16) for w in (ffn_w_gate, ffn_w_up, ffn_w_down))
    moe_g, moe_u, moe_d = (w.astype(BF16) for w in (moe_w_gate, moe_w_up, moe_w_down))
    router_pad = jnp.pad(router_w[0], ((0, 0), (0, LANES - N_EXPERTS))).astype(BF16)
    acat, lvl = _hgrn_constants()

    mod_all = _ada(jnp.concatenate([c_prompt, c_sample], axis=0), ada_w, ada_b)

    xp = x_prompt.reshape(bsz * seq, D_MODEL)
    xs = x_sample.reshape(n_s, D_MODEL)
    tm_p = 256
    hgrn_p, conv_p = [], []
    hgrn_s = conv_s = None
    for l in range(DEPTH):
        mod_p = mod_all[l, :bsz].reshape(bsz, 1, 6 * D_MODEL)
        mod_s = mod_all[l, bsz:].reshape(1, n_s, 6 * D_MODEL)

        zp, up, lfp = _inproj(xp, mod_p, norm1_g[l], lb_logits, w_in_b, l, tm_p, seq, BF16)
        zs, us, lfs = _inproj(xs, mod_s, norm1_g[l], lb_logits, w_in_b, l, n_s, n_s, F32)

        zp3 = zp.reshape(bsz, seq, N_GROUPS_OUT * D_MODEL)
        up3 = up.reshape(bsz, seq, D_MODEL)
        oa_p, s_p = _hgrn_prompt(zp3, lfp.reshape(bsz, seq, D_MODEL), hg_norm_g[l], acat, lvl, 512, 2)
        oa_s, hgrn_s = _hgrn_sample(zs, lfs, hg_norm_g[l], state_hgrn, l, hgrn_s, 8)

        yc_p = _conv_prompt(up3, conv_w[l], conv_b[l], 256)
        yc_s, conv_s = _conv_sample(us, state_conv, l, conv_s, conv_w[l], conv_b[l], 8)
        c_p = up3[:, seq - hist:, :]

        merge_w = (conv_ln_g[l], conv_ln_b[l], norm2_g[l], wco_b, whg_b, wo_b, l)
        dense = l % 2 == 0
        x1p, h2p = _merge(yc_p.reshape(bsz * seq, D_MODEL), oa_p.reshape(bsz * seq, D_MODEL),
                          zp, xp, mod_p, *merge_w, 512, seq, BF16 if dense else F32)
        x1s, h2s = _merge(yc_s, oa_s, zs, xs, mod_s, *merge_w, n_s, n_s)

        if dense:
            xp = _ffn(h2p, x1p, mod_p, ffn_g, ffn_u, ffn_d, 512, 256, seq)
            xs = _ffn(h2s, x1s, mod_s, ffn_g, ffn_u, ffn_d, n_s, 256, n_s)
        else:
            xp = _moe_routed(h2p, x1p, mod_p, router_pad, moe_g, moe_u, moe_d, final_norm_g,
                             512, 256, 256, seq)
            xs = _moe(h2s, x1s, mod_s, router_pad, moe_g, moe_u, moe_d, final_norm_g, n_s, 1408, n_s)

        hgrn_p.append(s_p)
        conv_p.append(c_p)

    return (xp.reshape(bsz, seq, D_MODEL), xs.reshape(n_s, 1, D_MODEL),
            jnp.stack(hgrn_p), jnp.stack(conv_p), hgrn_s, conv_s)
```

```python
import functools

import numpy as np
import jax
import jax.numpy as jnp
from jax import lax
from jax.experimental import pallas as pl
from jax.experimental.pallas import tpu as pltpu

F32 = jnp.float32
BF16 = jnp.bfloat16

D_MODEL = 1024
DEPTH = 2
HEADS = 8
HEAD_DIM = 128
CHUNK = 64
CONV_W = 31
N_EXPERTS = 8
D_FF = 2816
EPS = 1e-6
LOG2_E = 1.4426950408889634
LANES = 128
N_GROUPS_OUT = 6
VMEM_LIMIT = 48 * 1024 * 1024
MOE_VMEM_LIMIT = 56 * 1024 * 1024

LEVEL_WIDTHS = tuple(CHUNK >> (i + 1) for i in range(CHUNK.bit_length() - 1))


def _sigmoid(x):
    return 1.0 / (1.0 + jnp.exp(-x))


def _silu(x):
    return x * _sigmoid(x)


def _cparams(sem):
    return pltpu.CompilerParams(dimension_semantics=sem, vmem_limit_bytes=VMEM_LIMIT)


def _ada_kernel(c_ref, w_ref, b_ref, o_ref):
    a = _silu(c_ref[...]).astype(BF16)
    o_ref[...] = jnp.dot(a, w_ref[...].astype(BF16), preferred_element_type=F32) + b_ref[...]


def _ada(c_all, ada_w, ada_b):
    n = c_all.shape[0]
    tn = 1024
    return pl.pallas_call(
        _ada_kernel,
        grid=(DEPTH, 6 * D_MODEL // tn),
        in_specs=[
            pl.BlockSpec((n, D_MODEL), lambda l, j: (0, 0)),
            pl.BlockSpec((None, D_MODEL, tn), lambda l, j: (l, 0, j)),
            pl.BlockSpec((None, 1, tn), lambda l, j: (l, 0, j)),
        ],
        out_specs=pl.BlockSpec((None, n, tn), lambda l, j: (l, 0, j)),
        out_shape=jax.ShapeDtypeStruct((DEPTH, n, 6 * D_MODEL), F32),
        compiler_params=_cparams(("parallel", "parallel")),
        name="ada",
    )(c_all, ada_w, ada_b.reshape(DEPTH, 1, 6 * D_MODEL))


def _inproj_kernel(x_ref, sh_ref, sc_ref, g_ref, lbl_ref, w_ref, z_ref, u_ref, lf_ref, *, layer):
    x = x_ref[...]
    y = x * lax.rsqrt(jnp.mean(x * x, axis=-1, keepdims=True) + EPS) * g_ref[...]
    h = (y * (1.0 + sc_ref[...]) + sh_ref[...]).astype(BF16)

    def proj(group):
        return jnp.dot(h, w_ref[:, group * D_MODEL:(group + 1) * D_MODEL],
                       preferred_element_type=F32)

    def put(group, value):
        z_ref[:, group * D_MODEL:(group + 1) * D_MODEL] = value.astype(z_ref.dtype)

    put(0, _silu(proj(0)))
    lbl = lbl_ref[...]
    e = jnp.exp(lbl - jnp.max(lbl, axis=0, keepdims=True))
    p = e / jnp.sum(e, axis=0, keepdims=True)
    cum = p[0:1]
    for i in range(1, layer + 1):
        cum = cum + p[i:i + 1]
    lb = cum - p[0:1]
    zf = proj(1)
    put(1, (1.0 - lb) * _sigmoid(-zf))
    lf_ref[...] = jnp.log(lb + (1.0 - lb) * _sigmoid(zf))
    put(2, proj(2))
    put(3, _silu(proj(3)))
    u_ref[...] = proj(4) * _sigmoid(proj(5))
    put(4, _sigmoid(proj(6)))
    put(5, _sigmoid(proj(7)))


def _inproj(x, mod, norm_g, lb_logits, w_in_b, layer, tm, rows_per_mod, z_dtype):
    m = x.shape[0]
    mod_rows = mod.shape[1]
    n_cols = w_in_b.shape[-1]

    def mod_spec(col):
        return pl.BlockSpec((None, mod_rows, D_MODEL),
                            lambda i: ((i * tm) // rows_per_mod, 0, col))

    return pl.pallas_call(
        functools.partial(_inproj_kernel, layer=layer),
        grid=(m // tm,),
        in_specs=[
            pl.BlockSpec((tm, D_MODEL), lambda i: (i, 0)),
            mod_spec(0), mod_spec(1),
            pl.BlockSpec((1, D_MODEL), lambda i: (0, 0)),
            pl.BlockSpec((DEPTH, D_MODEL), lambda i: (0, 0)),
            pl.BlockSpec((None, D_MODEL, n_cols), lambda i: (layer, 0, 0),
                         pipeline_mode=pl.Buffered(1)),
        ],
        out_specs=[
            pl.BlockSpec((tm, N_GROUPS_OUT * D_MODEL), lambda i: (i, 0)),
            pl.BlockSpec((tm, D_MODEL), lambda i: (i, 0)),
            pl.BlockSpec((tm, D_MODEL), lambda i: (i, 0)),
        ],
        out_shape=[
            jax.ShapeDtypeStruct((m, N_GROUPS_OUT * D_MODEL), z_dtype),
            jax.ShapeDtypeStruct((m, D_MODEL), F32),
            jax.ShapeDtypeStruct((m, D_MODEL), F32),
        ],
        compiler_params=_cparams(("parallel",)),
        name="inproj",
    )(x, mod, mod, norm_g.reshape(1, D_MODEL), lb_logits, w_in_b)


def _hgrn_constants():
    t = np.arange(CHUNK)
    s = t[None, :]
    blocks = [(s <= t[:, None])]
    for w in LEVEL_WIDTHS:
        ref_row = ((t & ~(2 * w - 1)) + w - 1)[:, None]
        is_query = ((t & w) != 0)[:, None]
        blocks.append(np.where(is_query, (s > ref_row) & (s <= t[:, None]),
                               (s > t[:, None]) & (s <= ref_row)))
    cum = np.concatenate(blocks, axis=0).astype(np.float32)
    acat = np.concatenate([cum, cum, cum, np.zeros_like(cum)], axis=1)
    x = t[:, None] ^ t[None, :]
    lvl = np.full((CHUNK, CHUNK), -1, np.int32)
    for li, w in enumerate(LEVEL_WIDTHS):
        lvl[((x // w) == 1) & ((t[:, None] & w) != 0)] = li
    lvl[t[:, None] == t[None, :]] = len(LEVEL_WIDTHS)
    return jnp.asarray(acat, BF16), jnp.asarray(np.tile(lvl, (1, 2)))


def _block_diag(a, b):
    z = jnp.zeros_like(a)
    return jnp.concatenate([jnp.concatenate([a, z], axis=1),
                            jnp.concatenate([z, b], axis=1)], axis=0)


def _hgrn_prompt_kernel(q_ref, k_ref, v_ref, og_ref, lf_ref, gn_ref, acat_ref, lvl_ref,
                        o_ref, s_ref, st_scr, ops_scr, dec_scr, *, n_chunks):
    tb = pl.program_id(1)
    seqs = range(st_scr.shape[0])

    @pl.when(tb == 0)
    def _():
        st_scr[...] = jnp.zeros_like(st_scr)

    lvl = lvl_ref[...]
    trans_b = (((1,), (1,)), ((), ()))
    trans_a = (((0,), (0,)), ((), ()))
    pair_w = 2 * HEAD_DIM
    n_pairs = HEADS // 2
    n_lvl = len(LEVEL_WIDTHS)

    def head(x, h):
        return x[:, h * HEAD_DIM:(h + 1) * HEAD_DIM]

    def pair_rows(x, p):
        return _block_diag(head(x, 2 * p), head(x, 2 * p + 1))

    def lanes(p):
        return slice(p * pair_w, (p + 1) * pair_w)

    def chunk_rows(c):
        return pl.ds(pl.multiple_of(c * CHUNK, CHUNK), CHUNK)

    qe_at, kd_at = 2 * n_lvl, 2 * n_lvl + 1

    def prepare(c, slot):
        rows = chunk_rows(c)
        for s in seqs:
            lf = lf_ref[s, rows, :] * LOG2_E
            hi = lf.astype(BF16)
            r1 = lf - hi.astype(F32)
            mid = r1.astype(BF16)
            lo = (r1 - mid.astype(F32)).astype(BF16)
            lf4 = jnp.concatenate([hi, mid, lo, jnp.zeros_like(hi)], axis=0)
            cums = jnp.dot(acat_ref[...], lf4, preferred_element_type=F32)
            qb = q_ref[s, rows, :]
            kb = k_ref[s, rows, :]
            for li in range(n_lvl):
                e = jnp.exp2(cums[(li + 1) * CHUNK:(li + 2) * CHUNK]).astype(BF16)
                ops_scr[slot, s, li] = qb * e
                ops_scr[slot, s, n_lvl + li] = kb * e
            b = cums[0:CHUNK]
            b_last = b[CHUNK - 1:CHUNK]
            ops_scr[slot, s, qe_at] = qb * jnp.exp2(b).astype(BF16)
            ops_scr[slot, s, kd_at] = kb * jnp.exp2(b_last - b).astype(BF16)
            dec_scr[slot, s] = jnp.broadcast_to(jnp.exp2(b_last), dec_scr.shape[2:])

    def consume(c, slot):
        rows = chunk_rows(c)
        qb = [q_ref[s, rows, :] for s in seqs]
        kb = [k_ref[s, rows, :] for s in seqs]
        v = [v_ref[s, rows, :] for s in seqs]
        att = [[jnp.where(lvl == n_lvl,
                          lax.dot_general(qb[s][:, lanes(p)], pair_rows(kb[s], p), trans_b,
                                          preferred_element_type=F32), 0.0)
                for p in range(n_pairs)] for s in seqs]
        for li in range(n_lvl):
            for s in seqs:
                xq = ops_scr[slot, s, li]
                xk = ops_scr[slot, s, n_lvl + li]
                for p in range(n_pairs):
                    g = lax.dot_general(xq[:, lanes(p)], pair_rows(xk, p), trans_b,
                                        preferred_element_type=F32)
                    att[s][p] = jnp.where(lvl == li, g, att[s][p])
        inter = []
        for s in seqs:
            qe = ops_scr[slot, s, qe_at]
            inter.append([lax.dot_general(
                qe[:, lanes(p)],
                _block_diag(st_scr[s, 2 * p].astype(BF16), st_scr[s, 2 * p + 1].astype(BF16)),
                trans_b, preferred_element_type=F32) for p in range(n_pairs)])
        for s in seqs:
            kd = ops_scr[slot, s, kd_at]
            decay = dec_scr[slot, s, 0:1, :]
            for h in range(HEADS):
                st_scr[s, h] = head(decay, h) * st_scr[s, h] + lax.dot_general(
                    head(v[s], h), head(kd, h), trans_a, preferred_element_type=F32)
        for s in seqs:
            for p in range(n_pairs):
                o_pair = inter[s][p] + jnp.dot(att[s][p].astype(BF16), pair_rows(v[s], p),
                                               preferred_element_type=F32)
                for h in (2 * p, 2 * p + 1):
                    sl = slice(h * HEAD_DIM, (h + 1) * HEAD_DIM)
                    o = head(o_pair, h % 2)
                    o = o * lax.rsqrt(jnp.mean(o * o, axis=-1, keepdims=True) + EPS)
                    o = o * gn_ref[:, sl] * og_ref[s, rows, sl].astype(F32)
                    o_ref[s, rows, sl] = o.astype(BF16)

    assert n_chunks % 2 == 0
    prepare(0, 0)

    def chunk_pair(j, carry):
        prepare(2 * j + 1, 1)
        consume(2 * j, 0)
        prepare(2 * j + 2, 0)
        consume(2 * j + 1, 1)
        return carry

    lax.fori_loop(0, n_chunks // 2 - 1, chunk_pair, 0)
    prepare(n_chunks - 1, 1)
    consume(n_chunks - 2, 0)
    consume(n_chunks - 1, 1)

    @pl.when(tb == pl.num_programs(1) - 1)
    def _():
        for s in seqs:
            for h in range(HEADS):
                s_ref[s, h] = st_scr[s, h].T


def _hgrn_prompt(z3, lf3, gn, acat, lvl, tb, nb):
    bsz, t, _ = z3.shape

    def zspec(col):
        return pl.BlockSpec((nb, tb, D_MODEL), lambda b, i: (b, i, col))

    return pl.pallas_call(
        functools.partial(_hgrn_prompt_kernel, n_chunks=tb // CHUNK),
        grid=(bsz // nb, t // tb),
        in_specs=[
            zspec(0), zspec(1), zspec(2), zspec(3),
            pl.BlockSpec((nb, tb, D_MODEL), lambda b, i: (b, i, 0)),
            pl.BlockSpec((1, D_MODEL), lambda b, i: (0, 0)),
            pl.BlockSpec(acat.shape, lambda b, i: (0, 0)),
            pl.BlockSpec(lvl.shape, lambda b, i: (0, 0)),
        ],
        out_specs=[
            pl.BlockSpec((nb, tb, D_MODEL), lambda b, i: (b, i, 0)),
            pl.BlockSpec((nb, HEADS, HEAD_DIM, HEAD_DIM), lambda b, i: (b, 0, 0, 0)),
        ],
        out_shape=[
            jax.ShapeDtypeStruct((bsz, t, D_MODEL), BF16),
            jax.ShapeDtypeStruct((bsz, HEADS, HEAD_DIM, HEAD_DIM), F32),
        ],
        scratch_shapes=[
            pltpu.VMEM((nb, HEADS, HEAD_DIM, HEAD_DIM), F32),
            pltpu.VMEM((2, nb, 2 * len(LEVEL_WIDTHS) + 2, CHUNK, D_MODEL), BF16),
            pltpu.VMEM((2, nb, 8, D_MODEL), F32),
        ],
        compiler_params=_cparams(("parallel", "arbitrary")),
        name="hgrn_prompt",
    )(z3, z3, z3, z3, lf3, gn.reshape(1, D_MODEL), acat, lvl)


def _hgrn_sample_kernel(q_ref, k_ref, v_ref, og_ref, lf_ref, gn_ref, s_ref, *rest, bb):
    o_ref, so_ref, o_scr = rest[-3:]
    n = bb * HEADS

    @pl.when(pl.program_id(0) == 0)
    def _():
        qb = q_ref[...].reshape(n, HEAD_DIM).astype(BF16)
        kt = k_ref[...].reshape(n, HEAD_DIM).T
        ft = jnp.exp(lf_ref[...].reshape(n, HEAD_DIM)).T
        for b in range(bb):
            for h in range(HEADS):
                c = b * HEADS + h
                s_new = ft[:, c:c + 1] * s_ref[b, h] + kt[:, c:c + 1] * v_ref[b, h:h + 1, :]
                so_ref[b, h] = s_new
                o_scr[c:c + 1, :] = jnp.dot(qb[c:c + 1, :], s_new.astype(BF16),
                                            preferred_element_type=F32)
        o = o_scr[...]
        o = o * lax.rsqrt(jnp.mean(o * o, axis=-1, keepdims=True) + EPS)
        o = o * gn_ref[...] * og_ref[...].reshape(n, HEAD_DIM)
        o_ref[...] = o.astype(BF16)

    @pl.when(pl.program_id(0) > 0)
    def _():
        so_ref[...] = jnp.zeros_like(so_ref)


def _stacked_state_io(states, layer, stacked, block):
    n_blocks = states.shape[1] // block[0]
    zeros = (0,) * (len(block) - 1)
    blk = lambda s, i: jnp.where(s == 0, i, n_blocks - 1)
    in_specs = [pl.BlockSpec((None,) + block, lambda s, i: (layer, blk(s, i)) + zeros)]
    operands = [states]
    out_spec = pl.BlockSpec((None,) + block, lambda s, i: (layer + s, i) + zeros)
    if stacked is None:
        return states.shape[0] - layer, blk, in_specs, operands, out_spec
    in_specs.append(pl.BlockSpec(memory_space=pl.ANY))
    operands.append(stacked)
    return 1, blk, in_specs, operands, out_spec


def _hgrn_sample(z, lf, gn, states, layer, stacked, bb):
    n = z.shape[0]
    z3 = z.reshape(n, N_GROUPS_OUT * HEADS, HEAD_DIM)
    lf3 = lf.reshape(n, HEADS, HEAD_DIM)
    gn_rows = jnp.tile(gn.reshape(HEADS, HEAD_DIM), (bb, 1))
    rows, blk, state_in_specs, state_operands, state_out_spec = _stacked_state_io(
        states, layer, stacked, (bb, HEADS, HEAD_DIM, HEAD_DIM))

    def zspec(col):
        return pl.BlockSpec((bb, HEADS, HEAD_DIM), lambda s, i: (blk(s, i), col, 0))

    n_in = 6 + len(state_operands)
    o, s_new = pl.pallas_call(
        functools.partial(_hgrn_sample_kernel, bb=bb),
        grid=(rows, n // bb),
        in_specs=[
            zspec(0), zspec(1), zspec(2), zspec(3),
            pl.BlockSpec((bb, HEADS, HEAD_DIM), lambda s, i: (blk(s, i), 0, 0)),
            pl.BlockSpec((bb * HEADS, HEAD_DIM), lambda s, i: (0, 0)),
        ] + state_in_specs,
        out_specs=[
            pl.BlockSpec((bb * HEADS, HEAD_DIM), lambda s, i: (blk(s, i), 0)),
            state_out_spec,
        ],
        out_shape=[
            jax.ShapeDtypeStruct((n * HEADS, HEAD_DIM), BF16),
            jax.ShapeDtypeStruct(states.shape, F32),
        ],
        scratch_shapes=[pltpu.VMEM((bb * HEADS, HEAD_DIM), F32)],
        input_output_aliases={} if stacked is None else {n_in - 1: 1},
        compiler_params=_cparams(("arbitrary", "arbitrary")),
        name="hgrn_sample",
    )(z3, z3, z3, z3, lf3, gn_rows, *state_operands)
    return o.reshape(n, D_MODEL), s_new


CONV_HIST = 32
CONV_ROWS = 32


def _conv_prompt_kernel(u_ref, wb_ref, bias_ref, y_ref, buf, shifted, *, tc):
    @pl.when(pl.program_id(1) == 0)
    def _():
        buf[0:CONV_HIST, :] = jnp.zeros((CONV_HIST, D_MODEL), F32)

    buf[CONV_HIST:CONV_HIST + tc, :] = u_ref[...]
    lead = CONV_HIST - (CONV_W - 1)
    n_shift = shifted.shape[1]
    for s in range(1, 8):
        shifted[s - 1] = buf[s:s + n_shift, :]

    def rows_body(r, carry):
        base = pl.multiple_of(r * CONV_ROWS, CONV_ROWS)
        accs = [jnp.broadcast_to(bias_ref[...], (8, D_MODEL)) for _ in range(CONV_ROWS // 8)]
        for j in range(CONV_W):
            wj = wb_ref[j * 8:(j + 1) * 8, :]
            p, s = divmod(lead + j, 8)
            for a in range(CONV_ROWS // 8):
                rows = pl.ds(base + 8 * (a + p), 8)
                tap = buf[rows, :] if s == 0 else shifted[s - 1, rows, :]
                accs[a] = accs[a] + wj * tap
        for a in range(CONV_ROWS // 8):
            y_ref[pl.ds(base + a * 8, 8), :] = accs[a]
        return carry

    lax.fori_loop(0, tc // CONV_ROWS, rows_body, 0)
    buf[0:CONV_HIST, :] = buf[tc:tc + CONV_HIST, :]


def _conv_prompt(z3, conv_w, conv_b, tc):
    bsz, t, _ = z3.shape
    wb = jnp.repeat(conv_w, 8, axis=0)
    return pl.pallas_call(
        functools.partial(_conv_prompt_kernel, tc=tc),
        grid=(bsz, t // tc),
        in_specs=[
            pl.BlockSpec((None, tc, D_MODEL), lambda b, i: (b, i, 0)),
            pl.BlockSpec((CONV_W * 8, D_MODEL), lambda b, i: (0, 0)),
            pl.BlockSpec((1, D_MODEL), lambda b, i: (0, 0)),
        ],
        out_specs=pl.BlockSpec((None, tc, D_MODEL), lambda b, i: (b, i, 0)),
        out_shape=jax.ShapeDtypeStruct((bsz, t, D_MODEL), F32),
        scratch_shapes=[pltpu.VMEM((CONV_HIST + tc, D_MODEL), F32),
                        pltpu.VMEM((7, CONV_HIST + tc - 8, D_MODEL), F32)],
        compiler_params=_cparams(("parallel", "arbitrary")),
        name="conv_prompt",
    )(z3, wb, conv_b.reshape(1, D_MODEL))


def _conv_sample_kernel(u_ref, w_ref, bias_ref, s_ref, *rest, bb):
    y_ref, so_ref = rest[-2:]
    hist = CONV_W - 1

    @pl.when(pl.program_id(0) == 0)
    def _():
        for b in range(bb):
            u = u_ref[b:b + 1, :]
            y = jnp.sum(s_ref[b] * w_ref[0:hist, :], axis=0, keepdims=True)
            y_ref[b:b + 1, :] = y + w_ref[hist:CONV_W, :] * u + bias_ref[...]
            so_ref[b, 0:hist - 1, :] = s_ref[b, 1:hist, :]
            so_ref[b, hist - 1:hist, :] = u

    @pl.when(pl.program_id(0) > 0)
    def _():
        so_ref[...] = jnp.zeros_like(so_ref)


def _conv_sample(z, states, layer, stacked, conv_w, conv_b, bb):
    n = z.shape[0]
    rows, blk, state_in_specs, state_operands, state_out_spec = _stacked_state_io(
        states, layer, stacked, (bb, CONV_W - 1, D_MODEL))
    n_in = 3 + len(state_operands)
    return pl.pallas_call(
        functools.partial(_conv_sample_kernel, bb=bb),
        grid=(rows, n // bb),
        in_specs=[
            pl.BlockSpec((bb, D_MODEL), lambda s, i: (blk(s, i), 0)),
            pl.BlockSpec((CONV_W, D_MODEL), lambda s, i: (0, 0)),
            pl.BlockSpec((1, D_MODEL), lambda s, i: (0, 0)),
        ] + state_in_specs,
        out_specs=[pl.BlockSpec((bb, D_MODEL), lambda s, i: (blk(s, i), 0)), state_out_spec],
        out_shape=[jax.ShapeDtypeStruct((n, D_MODEL), F32),
                   jax.ShapeDtypeStruct(states.shape, F32)],
        input_output_aliases={} if stacked is None else {n_in - 1: 1},
        compiler_params=_cparams(("arbitrary", "arbitrary")),
        name="conv_sample",
    )(z, conv_w, conv_b.reshape(1, D_MODEL), *state_operands)


def _merge_kernel(yc_ref, oa_ref, ga_ref, gb_ref, x_ref, g1_ref, sh2_ref, sc2_ref,
                  lng_ref, lnb_ref, n2g_ref, wco_ref, whg_ref, wo_ref, x1_ref, h2_ref):
    yc = yc_ref[...]
    xc = yc - jnp.mean(yc, axis=-1, keepdims=True)
    vln = xc * lax.rsqrt(jnp.mean(xc * xc, axis=-1, keepdims=True) + EPS)
    va = _silu(vln * lng_ref[...] + lnb_ref[...]).astype(BF16)
    y_b = jnp.dot(va, wco_ref[...], preferred_element_type=F32)
    y_a = jnp.dot(oa_ref[...], whg_ref[...], preferred_element_type=F32)
    mixed = (ga_ref[...] * y_a + gb_ref[...] * y_b).astype(BF16)
    x1 = x_ref[...] + g1_ref[...] * jnp.dot(mixed, wo_ref[...], preferred_element_type=F32)
    x1_ref[...] = x1
    y = x1 * lax.rsqrt(jnp.mean(x1 * x1, axis=-1, keepdims=True) + EPS) * n2g_ref[...]
    h2_ref[...] = (y * (1.0 + sc2_ref[...]) + sh2_ref[...]).astype(h2_ref.dtype)


def _merge(yc, oa, z, x, mod, ln_g, ln_b, n2_g, wco, whg, wo, layer, tm, rows_per_mod,
           h2_dtype=BF16):
    m = x.shape[0]
    mod_rows = mod.shape[1]

    def mod_spec(col):
        return pl.BlockSpec((None, mod_rows, D_MODEL),
                            lambda i: ((i * tm) // rows_per_mod, 0, col))

    def row_spec(col=0):
        return pl.BlockSpec((tm, D_MODEL), lambda i: (i, col))

    vec_spec = pl.BlockSpec((1, D_MODEL), lambda i: (0, 0))
    w_spec = pl.BlockSpec((None, D_MODEL, D_MODEL), lambda i: (layer, 0, 0))
    return pl.pallas_call(
        _merge_kernel,
        grid=(m // tm,),
        in_specs=[row_spec(), row_spec(), row_spec(4), row_spec(5), row_spec(),
                  mod_spec(2), mod_spec(3), mod_spec(4),
                  vec_spec, vec_spec, vec_spec, w_spec, w_spec, w_spec],
        out_specs=[row_spec(), row_spec()],
        out_shape=[jax.ShapeDtypeStruct((m, D_MODEL), F32),
                   jax.ShapeDtypeStruct((m, D_MODEL), h2_dtype)],
        compiler_params=_cparams(("parallel",)),
        name="merge",
    )(yc, oa, z, z, x, mod, mod, mod,
      ln_g.reshape(1, D_MODEL), ln_b.reshape(1, D_MODEL), n2_g.reshape(1, D_MODEL),
      wco, whg, wo)


def _swiglu_rows(x, wg_ref, wu_ref, wd_ref, tf, between=None):
    acc = None
    for j in range(D_FF // tf):
        if between is not None:
            between(j)
        cols = slice(j * tf, (j + 1) * tf)
        a = _silu(jnp.dot(x, wg_ref[:, cols], preferred_element_type=F32))
        a = a * jnp.dot(x, wu_ref[:, cols], preferred_element_type=F32)
        d = jnp.dot(a.astype(BF16), wd_ref[cols, :], preferred_element_type=F32)
        acc = d if acc is None else acc + d
    return acc


def _ffn_kernel(h_ref, x1_ref, g2_ref, wg_ref, wu_ref, wd_ref, o_ref, *, tf):
    o_ref[...] = x1_ref[...] + g2_ref[...] * _swiglu_rows(h_ref[...], wg_ref, wu_ref, wd_ref, tf)


def _ffn(h2, x1, mod, wg, wu, wd, tm, tf, rows_per_mod):
    m = x1.shape[0]
    mod_rows = mod.shape[1]
    w_up_spec = pl.BlockSpec((None, D_MODEL, D_FF), lambda i: (0, 0, 0), pipeline_mode=pl.Buffered(1))
    return pl.pallas_call(
        functools.partial(_ffn_kernel, tf=tf),
        grid=(m // tm,),
        in_specs=[
            pl.BlockSpec((tm, D_MODEL), lambda i: (i, 0)),
            pl.BlockSpec((tm, D_MODEL), lambda i: (i, 0)),
            pl.BlockSpec((None, mod_rows, D_MODEL), lambda i: ((i * tm) // rows_per_mod, 0, 5)),
            w_up_spec, w_up_spec,
            pl.BlockSpec((None, D_FF, D_MODEL), lambda i: (0, 0, 0), pipeline_mode=pl.Buffered(1)),
        ],
        out_specs=pl.BlockSpec((tm, D_MODEL), lambda i: (i, 0)),
        out_shape=jax.ShapeDtypeStruct((m, D_MODEL), F32),
        compiler_params=_cparams(("parallel",)),
        name="ffn",
    )(h2, x1, mod, wg, wu, wd)


def _top2(logits, lane):
    neg = jnp.float32(-jnp.inf)
    logits = jnp.where(lane < N_EXPERTS, logits, neg)
    v1 = jnp.max(logits, axis=-1, keepdims=True)
    i1 = jnp.min(jnp.where(logits == v1, lane, LANES), axis=-1, keepdims=True)
    rest = jnp.where(lane == i1, neg, logits)
    v2 = jnp.max(rest, axis=-1, keepdims=True)
    i2 = jnp.min(jnp.where(rest == v2, lane, LANES), axis=-1, keepdims=True)
    e2 = jnp.exp(v2 - v1)
    w1 = 1.0 / (1.0 + e2)
    return i1, i2, w1, e2 * w1


def _route_kernel(h_ref, rwt_ref, r_ref):
    logits = lax.dot_general(rwt_ref[...], h_ref[...].astype(BF16), (((1,), (1,)), ((), ())),
                             preferred_element_type=F32)
    expert = lax.broadcasted_iota(jnp.int32, logits.shape, 0)
    neg = jnp.float32(-jnp.inf)
    v1 = jnp.max(logits, axis=0, keepdims=True)
    i1 = jnp.min(jnp.where(logits == v1, expert, N_EXPERTS), axis=0, keepdims=True)
    rest = jnp.where(expert == i1, neg, logits)
    v2 = jnp.max(rest, axis=0, keepdims=True)
    i2 = jnp.min(jnp.where(rest == v2, expert, N_EXPERTS), axis=0, keepdims=True)
    e2 = jnp.exp(v2 - v1)
    w1 = 1.0 / (1.0 + e2)
    packed = (jnp.where(expert == 0, i1.astype(F32), 0.0) + jnp.where(expert == 1, i2.astype(F32), 0.0)
              + jnp.where(expert == 2, w1, 0.0) + jnp.where(expert == 3, e2 * w1, 0.0))
    r_ref[...] = jnp.zeros_like(r_ref)
    r_ref[:, 0:N_EXPERTS] = packed.T


def _route(h2, router_t, tm):
    m = h2.shape[0]
    return pl.pallas_call(
        _route_kernel,
        grid=(m // tm,),
        in_specs=[pl.BlockSpec((tm, D_MODEL), lambda i: (i, 0)),
                  pl.BlockSpec((N_EXPERTS, D_MODEL), lambda i: (0, 0))],
        out_specs=pl.BlockSpec((tm, LANES), lambda i: (i, 0)),
        out_shape=jax.ShapeDtypeStruct((m, LANES), F32),
        compiler_params=_cparams(("parallel",)),
        name="route",
    )(h2, router_t)


def _start_row_gather(idx_ref, n_rows, src_hbm, dst, sem):
    for r in range(n_rows):
        pltpu.make_async_copy(src_hbm.at[pl.ds(idx_ref[r], 1), :], dst.at[pl.ds(r, 1), :], sem).start()


def _moe_dispatch_kernel(lo_ref, hi_ref, pos_ref, h_ref, x_hbm, stage, zero_row, sem, pad_sem):
    i = pl.program_id(0)
    last = pl.num_programs(0) - 1
    slot = lax.rem(i, 2)
    tb = stage.shape[1]

    def wait_block(s):
        for _ in range(2):
            pltpu.make_async_copy(stage.at[s], stage.at[s], sem.at[s]).wait()

    @pl.when(i >= 2)
    def _():
        wait_block(slot)

    stage[slot] = h_ref[...]

    for r in range(tb):
        row = stage.at[slot, pl.ds(r, 1), :]
        pltpu.make_async_copy(row, x_hbm.at[pl.ds(pos_ref[0, r], 1), :], sem.at[slot]).start()
        pltpu.make_async_copy(row, x_hbm.at[pl.ds(pos_ref[0, tb + r], 1), :], sem.at[slot]).start()

    @pl.when(i == last)
    def _():
        zero_row[...] = jnp.zeros_like(zero_row)
        pad_copy = lambda r: pltpu.make_async_copy(zero_row.at[pl.ds(0, 1), :],
                                                   x_hbm.at[pl.ds(r, 1), :], pad_sem.at[0])

        def start(r, carry):
            pad_copy(r).start()
            return carry

        def wait(r, carry):
            pad_copy(r).wait()
            return carry

        for e in range(lo_ref.shape[0]):
            lax.fori_loop(lo_ref[e], hi_ref[e], start, 0)
        for e in range(lo_ref.shape[0]):
            lax.fori_loop(lo_ref[e], hi_ref[e], wait, 0)
        wait_block(1 - slot)
        wait_block(slot)


def _moe_dispatch(h2, pos_blocks, pad_lo, pad_hi, n_rows):
    n_blocks = pos_blocks.shape[0]
    tb = pos_blocks.shape[2] // 2
    assert n_blocks >= 2
    grid_spec = pltpu.PrefetchScalarGridSpec(
        num_scalar_prefetch=2,
        grid=(n_blocks,),
        in_specs=[
            pl.BlockSpec((None, 1, 2 * tb), lambda i, lo, hi: (i, 0, 0), memory_space=pltpu.SMEM),
            pl.BlockSpec((tb, D_MODEL), lambda i, lo, hi: (i, 0)),
        ],
        out_specs=pl.BlockSpec(memory_space=pl.ANY),
        scratch_shapes=[pltpu.VMEM((2, tb, D_MODEL), F32), pltpu.VMEM((8, D_MODEL), F32),
                        pltpu.SemaphoreType.DMA((2,)), pltpu.SemaphoreType.DMA((1,))],
    )
    return pl.pallas_call(
        _moe_dispatch_kernel,
        grid_spec=grid_spec,
        out_shape=jax.ShapeDtypeStruct((n_rows, D_MODEL), F32),
        compiler_params=_cparams(("arbitrary",)),
        name="moe_dispatch",
    )(pad_lo, pad_hi, pos_blocks, h2)


def _moe_tiles_kernel(te_ref, na_ref, x_ref, wg_ref, wu_ref, wd_ref, y_ref, *, tf):
    @pl.when(pl.program_id(0) < na_ref[0])
    def _():
        y_ref[...] = _swiglu_rows(x_ref[...].astype(BF16), wg_ref, wu_ref, wd_ref, tf)

    @pl.when(pl.program_id(0) >= na_ref[0])
    def _():
        y_ref[...] = jnp.zeros_like(y_ref)


def _moe_tiles(x_sorted, tile_expert, n_active, wg, wu, wd, tm, tf):
    n_tiles = x_sorted.shape[0] // tm
    w_up_spec = pl.BlockSpec((None, None, D_MODEL, D_FF), lambda i, te, na: (0, te[i], 0, 0))
    grid_spec = pltpu.PrefetchScalarGridSpec(
        num_scalar_prefetch=2,
        grid=(n_tiles,),
        in_specs=[
            pl.BlockSpec((tm, D_MODEL), lambda i, te, na: (jnp.minimum(i, na[0] - 1), 0)),
            w_up_spec, w_up_spec,
            pl.BlockSpec((None, None, D_FF, D_MODEL), lambda i, te, na: (0, te[i], 0, 0)),
        ],
        out_specs=pl.BlockSpec((tm, D_MODEL), lambda i, te, na: (i, 0)),
    )
    return pl.pallas_call(
        functools.partial(_moe_tiles_kernel, tf=tf),
        grid_spec=grid_spec,
        out_shape=jax.ShapeDtypeStruct((n_tiles * tm, D_MODEL), F32),
        compiler_params=pltpu.CompilerParams(dimension_semantics=("arbitrary",),
                                             vmem_limit_bytes=MOE_VMEM_LIMIT),
        name="moe_tiles",
    )(tile_expert, n_active, x_sorted, wg, wu, wd)


def _moe_combine_kernel(pos_ref, pos_next_ref, y_hbm, r_ref, x1_ref, g2_ref, fg_ref, o_ref,
                        buf, sem):
    i = pl.program_id(0)
    slot = lax.rem(i, 2)
    n_rows = buf.shape[1]

    @pl.when(i == 0)
    def _():
        _start_row_gather(pos_ref.at[0], n_rows, y_hbm, buf.at[0], sem.at[0])

    @pl.when(i + 1 < pl.num_programs(0))
    def _():
        _start_row_gather(pos_next_ref.at[0], n_rows, y_hbm, buf.at[1 - slot], sem.at[1 - slot])

    pltpu.make_async_copy(buf.at[slot], buf.at[slot], sem.at[slot]).wait()
    tc = n_rows // 2
    lane = lax.broadcasted_iota(jnp.int32, r_ref.shape, 1)
    r = r_ref[...]
    w1 = jnp.sum(jnp.where(lane == 2, r, 0.0), axis=-1, keepdims=True)
    w2 = jnp.sum(jnp.where(lane == 3, r, 0.0), axis=-1, keepdims=True)
    f = w1 * buf[slot, 0:tc, :] + w2 * buf[slot, tc:n_rows, :]
    x2 = x1_ref[...] + g2_ref[...] * f
    o_ref[...] = x2 * lax.rsqrt(jnp.mean(x2 * x2, axis=-1, keepdims=True) + EPS) * fg_ref[...]


def _moe_combine(y_sorted, pos, route, x1, mod, final_g, tc, rows_per_mod):
    m = x1.shape[0]
    n_blocks = m // tc
    mod_rows = mod.shape[1]
    pos_spec = lambda f: pl.BlockSpec((None, 1, 2 * tc), f, memory_space=pltpu.SMEM)
    return pl.pallas_call(
        _moe_combine_kernel,
        grid=(n_blocks,),
        in_specs=[
            pos_spec(lambda i: (i, 0, 0)),
            pos_spec(lambda i: (jnp.minimum(i + 1, n_blocks - 1), 0, 0)),
            pl.BlockSpec(memory_space=pl.ANY),
            pl.BlockSpec((tc, LANES), lambda i: (i, 0)),
            pl.BlockSpec((tc, D_MODEL), lambda i: (i, 0)),
            pl.BlockSpec((None, mod_rows, D_MODEL), lambda i: ((i * tc) // rows_per_mod, 0, 5)),
            pl.BlockSpec((1, D_MODEL), lambda i: (0, 0)),
        ],
        out_specs=pl.BlockSpec((tc, D_MODEL), lambda i: (i, 0)),
        out_shape=jax.ShapeDtypeStruct((m, D_MODEL), F32),
        scratch_shapes=[pltpu.VMEM((2, 2 * tc, D_MODEL), F32), pltpu.SemaphoreType.DMA((2,))],
        compiler_params=_cparams(("arbitrary",)),
        name="moe_combine",
    )(pos, pos, y_sorted, route, x1, mod, final_g.reshape(1, D_MODEL))


def _moe_routed(h2, x1, mod, router_t, wg, wu, wd, final_g, tm, tf, tc, rows_per_mod):
    n = h2.shape[0]
    route = _route(h2, router_t, 512)
    expert = route[:, :2].astype(jnp.int32).reshape(2 * n)
    onehot = (expert[:, None] == jnp.arange(N_EXPERTS, dtype=jnp.int32)[None, :]).astype(jnp.int32)
    running = jnp.cumsum(onehot, axis=0)
    rank = jnp.sum((running - onehot) * onehot, axis=1)
    count = running[-1]
    padded = ((count + tm - 1) // tm) * tm
    group_end = jnp.cumsum(padded)
    group_start = group_end - padded
    pos = group_start[expert] + rank
    n_tiles = (2 * n) // tm + N_EXPERTS
    n_rows = n_tiles * tm
    tile_start = jnp.arange(n_tiles, dtype=jnp.int32) * tm
    tile_expert = jnp.minimum(jnp.sum(tile_start[:, None] >= group_end[None, :], axis=1),
                              N_EXPERTS - 1).astype(jnp.int32)
    n_active = (group_end[-1:] // tm).astype(jnp.int32)
    pad_lo = jnp.concatenate([group_start + count, group_end[-1:]]).astype(jnp.int32)
    pad_hi = jnp.concatenate([group_end, jnp.full((1,), n_rows)]).astype(jnp.int32)
    pos_blocks = pos.reshape(n // tc, tc, 2).transpose(0, 2, 1).reshape(n // tc, 1, 2 * tc)

    x_sorted = _moe_dispatch(h2, pos_blocks, pad_lo, pad_hi, n_rows)
    y_sorted = _moe_tiles(x_sorted, tile_expert, n_active, wg, wu, wd, tm, tf)
    return _moe_combine(y_sorted, pos_blocks, route, x1, mod, final_g, tc, rows_per_mod)


def _moe_kernel(h_ref, x1_ref, g2_ref, rw_ref, wg_ref, wu_ref, wd_ref, fg_ref, o_ref,
                acc, comb):
    e = pl.program_id(1)
    j = pl.program_id(2)
    lane = lax.broadcasted_iota(jnp.int32, comb.shape, 1)

    @pl.when((e == 0) & (j == 0))
    def _():
        acc[...] = jnp.zeros_like(acc)
        i1, i2, w1, w2 = _top2(jnp.dot(h_ref[...], rw_ref[...], preferred_element_type=F32), lane)
        comb[...] = jnp.where(lane == i1, w1, 0.0) + jnp.where(lane == i2, w2, 0.0)

    ce = jnp.sum(jnp.where(lane == e, comb[...], 0.0), axis=-1, keepdims=True)
    h = h_ref[...]
    a = _silu(jnp.dot(h, wg_ref[...], preferred_element_type=F32))
    a = a * jnp.dot(h, wu_ref[...], preferred_element_type=F32) * ce
    acc[...] += jnp.dot(a.astype(BF16), wd_ref[...], preferred_element_type=F32)

    @pl.when((e == pl.num_programs(1) - 1) & (j == pl.num_programs(2) - 1))
    def _():
        x2 = x1_ref[...] + g2_ref[...] * acc[...]
        o_ref[...] = x2 * lax.rsqrt(jnp.mean(x2 * x2, axis=-1, keepdims=True) + EPS) * fg_ref[...]


def _moe(h2, x1, mod, router_pad, wg, wu, wd, final_g, tm, tf, rows_per_mod):
    m = x1.shape[0]
    mod_rows = mod.shape[1]
    return pl.pallas_call(
        _moe_kernel,
        grid=(m // tm, N_EXPERTS, D_FF // tf),
        in_specs=[
            pl.BlockSpec((tm, D_MODEL), lambda i, e, j: (i, 0)),
            pl.BlockSpec((tm, D_MODEL), lambda i, e, j: (i, 0)),
            pl.BlockSpec((None, mod_rows, D_MODEL),
                         lambda i, e, j: ((i * tm) // rows_per_mod, 0, 5)),
            pl.BlockSpec((D_MODEL, LANES), lambda i, e, j: (0, 0)),
            pl.BlockSpec((None, None, D_MODEL, tf), lambda i, e, j: (0, e, 0, j)),
            pl.BlockSpec((None, None, D_MODEL, tf), lambda i, e, j: (0, e, 0, j)),
            pl.BlockSpec((None, None, tf, D_MODEL), lambda i, e, j: (0, e, j, 0)),
            pl.BlockSpec((1, D_MODEL), lambda i, e, j: (0, 0)),
        ],
        out_specs=pl.BlockSpec((tm, D_MODEL), lambda i, e, j: (i, 0)),
        out_shape=jax.ShapeDtypeStruct((m, D_MODEL), F32),
        scratch_shapes=[pltpu.VMEM((tm, D_MODEL), F32), pltpu.VMEM((tm, LANES), F32)],
        compiler_params=_cparams(("parallel", "arbitrary", "arbitrary")),
        name="moe",
    )(h2, x1, mod, router_pad, wg, wu, wd, final_g.reshape(1, D_MODEL))


def kernel(x_prompt, x_sample, state_hgrn, state_conv, c_prompt, c_sample, ada_w, ada_b, norm1_g, norm2_g, w_in, lb_logits, hg_norm_g, w_hg_out, conv_w, conv_b, conv_ln_g, conv_ln_b, w_conv_out, w_o, ffn_w_gate, ffn_w_up, ffn_w_down, router_w, moe_w_gate, moe_w_up, moe_w_down, final_norm_g):
    bsz, seq, _ = x_prompt.shape
    n_s = x_sample.shape[0]
    hist = CONV_W - 1

    w_in_b = w_in.astype(BF16)
    whg_b = w_hg_out.astype(BF16)
    wco_b = w_conv_out.astype(BF16)
    wo_b = w_o.astype(BF16)
    ffn_g, ffn_u, ffn_d = (w.astype(BF16) for w in (ffn_w_gate, ffn_w_up, ffn_w_down))
    moe_g, moe_u, moe_d = (w.astype(BF16) for w in (moe_w_gate, moe_w_up, moe_w_down))
    router_pad = jnp.pad(router_w[0], ((0, 0), (0, LANES - N_EXPERTS))).astype(BF16)
    router_t = router_w[0].T.astype(BF16)
    acat, lvl = _hgrn_constants()

    mod_all = _ada(jnp.concatenate([c_prompt, c_sample], axis=0), ada_w, ada_b)

    xp = x_prompt.reshape(bsz * seq, D_MODEL)
    xs = x_sample.reshape(n_s, D_MODEL)
    tm_p = 256
    hgrn_p, conv_p = [], []
    hgrn_s = conv_s = None
    for l in range(DEPTH):
        mod_p = mod_all[l, :bsz].reshape(bsz, 1, 6 * D_MODEL)
        mod_s = mod_all[l, bsz:].reshape(1, n_s, 6 * D_MODEL)

        zp, up, lfp = _inproj(xp, mod_p, norm1_g[l], lb_logits, w_in_b, l, tm_p, seq, BF16)
        zs, us, lfs = _inproj(xs, mod_s, norm1_g[l], lb_logits, w_in_b, l, n_s, n_s, F32)

        zp3 = zp.reshape(bsz, seq, N_GROUPS_OUT * D_MODEL)
        up3 = up.reshape(bsz, seq, D_MODEL)
        oa_p, s_p = _hgrn_prompt(zp3, lfp.reshape(bsz, seq, D_MODEL), hg_norm_g[l], acat, lvl, 512, 2)
        oa_s, hgrn_s = _hgrn_sample(zs, lfs, hg_norm_g[l], state_hgrn, l, hgrn_s, 8)

        yc_p = _conv_prompt(up3, conv_w[l], conv_b[l], 256)
        yc_s, conv_s = _conv_sample(us, state_conv, l, conv_s, conv_w[l], conv_b[l], 8)
        c_p = up3[:, seq - hist:, :]

        merge_w = (conv_ln_g[l], conv_ln_b[l], norm2_g[l], wco_b, whg_b, wo_b, l)
        dense = l % 2 == 0
        x1p, h2p = _merge(yc_p.reshape(bsz * seq, D_MODEL), oa_p.reshape(bsz * seq, D_MODEL),
                          zp, xp, mod_p, *merge_w, 512, seq, BF16 if dense else F32)
        x1s, h2s = _merge(yc_s, oa_s, zs, xs, mod_s, *merge_w, n_s, n_s)

        if dense:
            xp = _ffn(h2p, x1p, mod_p, ffn_g, ffn_u, ffn_d, 512, 256, seq)
            xs = _ffn(h2s, x1s, mod_s, ffn_g, ffn_u, ffn_d, n_s, 256, n_s)
        else:
            xp = _moe_routed(h2p, x1p, mod_p, router_t, moe_g, moe_u, moe_d, final_norm_g,
                             512, 256, 256, seq)
            xs = _moe(h2s, x1s, mod_s, router_pad, moe_g, moe_u, moe_d, final_norm_g, n_s, 1408, n_s)

        hgrn_p.append(s_p)
        conv_p.append(c_p)

    return (xp.reshape(bsz, seq, D_MODEL), xs.reshape(n_s, 1, D_MODEL),
            jnp.stack(hgrn_p), jnp.stack(conv_p), hgrn_s, conv_s)
```

```python
import functools

import numpy as np
import jax
import jax.numpy as jnp
from jax import lax
from jax.experimental import pallas as pl
from jax.experimental.pallas import tpu as pltpu

F32 = jnp.float32
BF16 = jnp.bfloat16

D_MODEL = 1024
DEPTH = 2
HEADS = 8
HEAD_DIM = 128
CHUNK = 64
CONV_W = 31
N_EXPERTS = 8
D_FF = 2816
EPS = 1e-6
LOG2_E = 1.4426950408889634
LANES = 128
N_GROUPS_OUT = 6
VMEM_LIMIT = 48 * 1024 * 1024
MOE_VMEM_LIMIT = 56 * 1024 * 1024

LEVEL_WIDTHS = tuple(CHUNK >> (i + 1) for i in range(CHUNK.bit_length() - 1))


def _sigmoid(x):
    return 1.0 / (1.0 + jnp.exp(-x))


def _silu(x):
    return x * _sigmoid(x)


def _cparams(sem):
    return pltpu.CompilerParams(dimension_semantics=sem, vmem_limit_bytes=VMEM_LIMIT)


def _ada_kernel(c_ref, w_ref, b_ref, o_ref):
    a = _silu(c_ref[...]).astype(BF16)
    o_ref[...] = jnp.dot(a, w_ref[...].astype(BF16), preferred_element_type=F32) + b_ref[...]


def _ada(c_all, ada_w, ada_b):
    n = c_all.shape[0]
    tn = 1024
    return pl.pallas_call(
        _ada_kernel,
        grid=(DEPTH, 6 * D_MODEL // tn),
        in_specs=[
            pl.BlockSpec((n, D_MODEL), lambda l, j: (0, 0)),
            pl.BlockSpec((None, D_MODEL, tn), lambda l, j: (l, 0, j)),
            pl.BlockSpec((None, 1, tn), lambda l, j: (l, 0, j)),
        ],
        out_specs=pl.BlockSpec((None, n, tn), lambda l, j: (l, 0, j)),
        out_shape=jax.ShapeDtypeStruct((DEPTH, n, 6 * D_MODEL), F32),
        compiler_params=_cparams(("parallel", "parallel")),
        name="ada",
    )(c_all, ada_w, ada_b.reshape(DEPTH, 1, 6 * D_MODEL))


def _inproj_kernel(x_ref, sh_ref, sc_ref, g_ref, lbl_ref, w_ref, z_ref, u_ref, lf_ref, *, layer):
    x = x_ref[...]
    y = x * lax.rsqrt(jnp.mean(x * x, axis=-1, keepdims=True) + EPS) * g_ref[...]
    h = (y * (1.0 + sc_ref[...]) + sh_ref[...]).astype(BF16)

    def proj(group):
        return jnp.dot(h, w_ref[:, group * D_MODEL:(group + 1) * D_MODEL],
                       preferred_element_type=F32)

    def put(group, value):
        z_ref[:, group * D_MODEL:(group + 1) * D_MODEL] = value.astype(z_ref.dtype)

    put(0, _silu(proj(0)))
    lbl = lbl_ref[...]
    e = jnp.exp(lbl - jnp.max(lbl, axis=0, keepdims=True))
    p = e / jnp.sum(e, axis=0, keepdims=True)
    cum = p[0:1]
    for i in range(1, layer + 1):
        cum = cum + p[i:i + 1]
    lb = cum - p[0:1]
    zf = proj(1)
    put(1, (1.0 - lb) * _sigmoid(-zf))
    lf_ref[...] = jnp.log(lb + (1.0 - lb) * _sigmoid(zf))
    put(2, proj(2))
    put(3, _silu(proj(3)))
    u_ref[...] = proj(4) * _sigmoid(proj(5))
    put(4, _sigmoid(proj(6)))
    put(5, _sigmoid(proj(7)))


def _inproj(x, mod, norm_g, lb_logits, w_in_b, layer, tm, rows_per_mod, z_dtype):
    m = x.shape[0]
    mod_rows = mod.shape[1]
    n_cols = w_in_b.shape[-1]

    def mod_spec(col):
        return pl.BlockSpec((None, mod_rows, D_MODEL),
                            lambda i: ((i * tm) // rows_per_mod, 0, col))

    return pl.pallas_call(
        functools.partial(_inproj_kernel, layer=layer),
        grid=(m // tm,),
        in_specs=[
            pl.BlockSpec((tm, D_MODEL), lambda i: (i, 0)),
            mod_spec(0), mod_spec(1),
            pl.BlockSpec((1, D_MODEL), lambda i: (0, 0)),
            pl.BlockSpec((DEPTH, D_MODEL), lambda i: (0, 0)),
            pl.BlockSpec((None, D_MODEL, n_cols), lambda i: (layer, 0, 0),
                         pipeline_mode=pl.Buffered(1)),
        ],
        out_specs=[
            pl.BlockSpec((tm, N_GROUPS_OUT * D_MODEL), lambda i: (i, 0)),
            pl.BlockSpec((tm, D_MODEL), lambda i: (i, 0)),
            pl.BlockSpec((tm, D_MODEL), lambda i: (i, 0)),
        ],
        out_shape=[
            jax.ShapeDtypeStruct((m, N_GROUPS_OUT * D_MODEL), z_dtype),
            jax.ShapeDtypeStruct((m, D_MODEL), F32),
            jax.ShapeDtypeStruct((m, D_MODEL), F32),
        ],
        compiler_params=_cparams(("parallel",)),
        name="inproj",
    )(x, mod, mod, norm_g.reshape(1, D_MODEL), lb_logits, w_in_b)


def _hgrn_constants():
    t = np.arange(CHUNK)
    s = t[None, :]
    blocks = [(s <= t[:, None])]
    for w in LEVEL_WIDTHS:
        ref_row = ((t & ~(2 * w - 1)) + w - 1)[:, None]
        is_query = ((t & w) != 0)[:, None]
        blocks.append(np.where(is_query, (s > ref_row) & (s <= t[:, None]),
                               (s > t[:, None]) & (s <= ref_row)))
    cum = np.concatenate(blocks, axis=0).astype(np.float32)
    acat = np.concatenate([cum, cum, cum, np.zeros_like(cum)], axis=1)
    x = t[:, None] ^ t[None, :]
    lvl = np.full((CHUNK, CHUNK), -1, np.int32)
    for li, w in enumerate(LEVEL_WIDTHS):
        lvl[((x // w) == 1) & ((t[:, None] & w) != 0)] = li
    lvl[t[:, None] == t[None, :]] = len(LEVEL_WIDTHS)
    return jnp.asarray(acat, BF16), jnp.asarray(np.tile(lvl, (1, 2)))


def _block_diag(a, b):
    z = jnp.zeros_like(a)
    return jnp.concatenate([jnp.concatenate([a, z], axis=1),
                            jnp.concatenate([z, b], axis=1)], axis=0)


def _hgrn_prompt_kernel(q_ref, k_ref, v_ref, og_ref, lf_ref, gn_ref, acat_ref, lvl_ref,
                        o_ref, s_ref, st_scr, ops_scr, dec_scr, *, n_chunks):
    tb = pl.program_id(1)
    seqs = range(st_scr.shape[0])

    @pl.when(tb == 0)
    def _():
        st_scr[...] = jnp.zeros_like(st_scr)

    lvl = lvl_ref[...]
    trans_b = (((1,), (1,)), ((), ()))
    trans_a = (((0,), (0,)), ((), ()))
    pair_w = 2 * HEAD_DIM
    n_pairs = HEADS // 2
    n_lvl = len(LEVEL_WIDTHS)

    def head(x, h):
        return x[:, h * HEAD_DIM:(h + 1) * HEAD_DIM]

    def pair_rows(x, p):
        return _block_diag(head(x, 2 * p), head(x, 2 * p + 1))

    def lanes(p):
        return slice(p * pair_w, (p + 1) * pair_w)

    def chunk_rows(c):
        return pl.ds(pl.multiple_of(c * CHUNK, CHUNK), CHUNK)

    qe_at, kd_at = 2 * n_lvl, 2 * n_lvl + 1

    def prepare(c, slot):
        rows = chunk_rows(c)
        for s in seqs:
            lf = lf_ref[s, rows, :] * LOG2_E
            hi = lf.astype(BF16)
            r1 = lf - hi.astype(F32)
            mid = r1.astype(BF16)
            lo = (r1 - mid.astype(F32)).astype(BF16)
            lf4 = jnp.concatenate([hi, mid, lo, jnp.zeros_like(hi)], axis=0)
            cums = jnp.dot(acat_ref[...], lf4, preferred_element_type=F32)
            qb = q_ref[s, rows, :]
            kb = k_ref[s, rows, :]
            for li in range(n_lvl):
                e = jnp.exp2(cums[(li + 1) * CHUNK:(li + 2) * CHUNK]).astype(BF16)
                ops_scr[slot, s, li] = qb * e
                ops_scr[slot, s, n_lvl + li] = kb * e
            b = cums[0:CHUNK]
            b_last = b[CHUNK - 1:CHUNK]
            ops_scr[slot, s, qe_at] = qb * jnp.exp2(b).astype(BF16)
            ops_scr[slot, s, kd_at] = kb * jnp.exp2(b_last - b).astype(BF16)
            dec_scr[slot, s] = jnp.broadcast_to(jnp.exp2(b_last), dec_scr.shape[2:])

    def consume(c, slot):
        rows = chunk_rows(c)
        qb = [q_ref[s, rows, :] for s in seqs]
        kb = [k_ref[s, rows, :] for s in seqs]
        v = [v_ref[s, rows, :] for s in seqs]
        att = [[jnp.where(lvl == n_lvl,
                          lax.dot_general(qb[s][:, lanes(p)], pair_rows(kb[s], p), trans_b,
                                          preferred_element_type=F32), 0.0)
                for p in range(n_pairs)] for s in seqs]
        for li in range(n_lvl):
            for s in seqs:
                xq = ops_scr[slot, s, li]
                xk = ops_scr[slot, s, n_lvl + li]
                for p in range(n_pairs):
                    g = lax.dot_general(xq[:, lanes(p)], pair_rows(xk, p), trans_b,
                                        preferred_element_type=F32)
                    att[s][p] = jnp.where(lvl == li, g, att[s][p])
        inter = []
        for s in seqs:
            qe = ops_scr[slot, s, qe_at]
            inter.append([lax.dot_general(
                qe[:, lanes(p)],
                _block_diag(st_scr[s, 2 * p].astype(BF16), st_scr[s, 2 * p + 1].astype(BF16)),
                trans_b, preferred_element_type=F32) for p in range(n_pairs)])
        for s in seqs:
            kd = ops_scr[slot, s, kd_at]
            decay = dec_scr[slot, s, 0:1, :]
            for h in range(HEADS):
                st_scr[s, h] = head(decay, h) * st_scr[s, h] + lax.dot_general(
                    head(v[s], h), head(kd, h), trans_a, preferred_element_type=F32)
        for s in seqs:
            for p in range(n_pairs):
                o_pair = inter[s][p] + jnp.dot(att[s][p].astype(BF16), pair_rows(v[s], p),
                                               preferred_element_type=F32)
                for h in (2 * p, 2 * p + 1):
                    sl = slice(h * HEAD_DIM, (h + 1) * HEAD_DIM)
                    o = head(o_pair, h % 2)
                    o = o * lax.rsqrt(jnp.mean(o * o, axis=-1, keepdims=True) + EPS)
                    o = o * gn_ref[:, sl] * og_ref[s, rows, sl].astype(F32)
                    o_ref[s, rows, sl] = o.astype(BF16)

    assert n_chunks % 2 == 0
    prepare(0, 0)

    def chunk_pair(j, carry):
        prepare(2 * j + 1, 1)
        consume(2 * j, 0)
        prepare(2 * j + 2, 0)
        consume(2 * j + 1, 1)
        return carry

    lax.fori_loop(0, n_chunks // 2 - 1, chunk_pair, 0)
    prepare(n_chunks - 1, 1)
    consume(n_chunks - 2, 0)
    consume(n_chunks - 1, 1)

    @pl.when(tb == pl.num_programs(1) - 1)
    def _():
        for s in seqs:
            for h in range(HEADS):
                s_ref[s, h] = st_scr[s, h].T


def _hgrn_prompt(z3, lf3, gn, acat, lvl, tb, nb):
    bsz, t, _ = z3.shape

    def zspec(col):
        return pl.BlockSpec((nb, tb, D_MODEL), lambda b, i: (b, i, col))

    return pl.pallas_call(
        functools.partial(_hgrn_prompt_kernel, n_chunks=tb // CHUNK),
        grid=(bsz // nb, t // tb),
        in_specs=[
            zspec(0), zspec(1), zspec(2), zspec(3),
            pl.BlockSpec((nb, tb, D_MODEL), lambda b, i: (b, i, 0)),
            pl.BlockSpec((1, D_MODEL), lambda b, i: (0, 0)),
            pl.BlockSpec(acat.shape, lambda b, i: (0, 0)),
            pl.BlockSpec(lvl.shape, lambda b, i: (0, 0)),
        ],
        out_specs=[
            pl.BlockSpec((nb, tb, D_MODEL), lambda b, i: (b, i, 0)),
            pl.BlockSpec((nb, HEADS, HEAD_DIM, HEAD_DIM), lambda b, i: (b, 0, 0, 0)),
        ],
        out_shape=[
            jax.ShapeDtypeStruct((bsz, t, D_MODEL), BF16),
            jax.ShapeDtypeStruct((bsz, HEADS, HEAD_DIM, HEAD_DIM), F32),
        ],
        scratch_shapes=[
            pltpu.VMEM((nb, HEADS, HEAD_DIM, HEAD_DIM), F32),
            pltpu.VMEM((2, nb, 2 * len(LEVEL_WIDTHS) + 2, CHUNK, D_MODEL), BF16),
            pltpu.VMEM((2, nb, 8, D_MODEL), F32),
        ],
        compiler_params=_cparams(("parallel", "arbitrary")),
        name="hgrn_prompt",
    )(z3, z3, z3, z3, lf3, gn.reshape(1, D_MODEL), acat, lvl)


def _hgrn_sample_kernel(q_ref, k_ref, v_ref, og_ref, lf_ref, gn_ref, s_ref, *rest, bb):
    o_ref, so_ref, o_scr = rest[-3:]
    n = bb * HEADS

    @pl.when(pl.program_id(0) == 0)
    def _():
        qb = q_ref[...].reshape(n, HEAD_DIM).astype(BF16)
        kt = k_ref[...].reshape(n, HEAD_DIM).T
        ft = jnp.exp(lf_ref[...].reshape(n, HEAD_DIM)).T
        for b in range(bb):
            for h in range(HEADS):
                c = b * HEADS + h
                s_new = ft[:, c:c + 1] * s_ref[b, h] + kt[:, c:c + 1] * v_ref[b, h:h + 1, :]
                so_ref[b, h] = s_new
                o_scr[c:c + 1, :] = jnp.dot(qb[c:c + 1, :], s_new.astype(BF16),
                                            preferred_element_type=F32)
        o = o_scr[...]
        o = o * lax.rsqrt(jnp.mean(o * o, axis=-1, keepdims=True) + EPS)
        o = o * gn_ref[...] * og_ref[...].reshape(n, HEAD_DIM)
        o_ref[...] = o.astype(BF16)

    @pl.when(pl.program_id(0) > 0)
    def _():
        so_ref[...] = jnp.zeros_like(so_ref)


def _stacked_state_io(states, layer, stacked, block):
    n_blocks = states.shape[1] // block[0]
    zeros = (0,) * (len(block) - 1)
    blk = lambda s, i: jnp.where(s == 0, i, n_blocks - 1)
    in_specs = [pl.BlockSpec((None,) + block, lambda s, i: (layer, blk(s, i)) + zeros)]
    operands = [states]
    out_spec = pl.BlockSpec((None,) + block, lambda s, i: (layer + s, i) + zeros)
    if stacked is None:
        return states.shape[0] - layer, blk, in_specs, operands, out_spec
    in_specs.append(pl.BlockSpec(memory_space=pl.ANY))
    operands.append(stacked)
    return 1, blk, in_specs, operands, out_spec


def _hgrn_sample(z, lf, gn, states, layer, stacked, bb):
    n = z.shape[0]
    z3 = z.reshape(n, N_GROUPS_OUT * HEADS, HEAD_DIM)
    lf3 = lf.reshape(n, HEADS, HEAD_DIM)
    gn_rows = jnp.tile(gn.reshape(HEADS, HEAD_DIM), (bb, 1))
    rows, blk, state_in_specs, state_operands, state_out_spec = _stacked_state_io(
        states, layer, stacked, (bb, HEADS, HEAD_DIM, HEAD_DIM))

    def zspec(col):
        return pl.BlockSpec((bb, HEADS, HEAD_DIM), lambda s, i: (blk(s, i), col, 0))

    n_in = 6 + len(state_operands)
    o, s_new = pl.pallas_call(
        functools.partial(_hgrn_sample_kernel, bb=bb),
        grid=(rows, n // bb),
        in_specs=[
            zspec(0), zspec(1), zspec(2), zspec(3),
            pl.BlockSpec((bb, HEADS, HEAD_DIM), lambda s, i: (blk(s, i), 0, 0)),
            pl.BlockSpec((bb * HEADS, HEAD_DIM), lambda s, i: (0, 0)),
        ] + state_in_specs,
        out_specs=[
            pl.BlockSpec((bb * HEADS, HEAD_DIM), lambda s, i: (blk(s, i), 0)),
            state_out_spec,
        ],
        out_shape=[
            jax.ShapeDtypeStruct((n * HEADS, HEAD_DIM), BF16),
            jax.ShapeDtypeStruct(states.shape, F32),
        ],
        scratch_shapes=[pltpu.VMEM((bb * HEADS, HEAD_DIM), F32)],
        input_output_aliases={} if stacked is None else {n_in - 1: 1},
        compiler_params=_cparams(("arbitrary", "arbitrary")),
        name="hgrn_sample",
    )(z3, z3, z3, z3, lf3, gn_rows, *state_operands)
    return o.reshape(n, D_MODEL), s_new


CONV_HIST = 32
CONV_ROWS = 32


def _conv_prompt_kernel(u_ref, wb_ref, bias_ref, y_ref, buf, shifted, *, tc):
    @pl.when(pl.program_id(1) == 0)
    def _():
        buf[0:CONV_HIST, :] = jnp.zeros((CONV_HIST, D_MODEL), F32)

    buf[CONV_HIST:CONV_HIST + tc, :] = u_ref[...]
    lead = CONV_HIST - (CONV_W - 1)
    n_shift = shifted.shape[1]
    for s in range(1, 8):
        shifted[s - 1] = buf[s:s + n_shift, :]

    def rows_body(r, carry):
        base = pl.multiple_of(r * CONV_ROWS, CONV_ROWS)
        accs = [jnp.broadcast_to(bias_ref[...], (8, D_MODEL)) for _ in range(CONV_ROWS // 8)]
        for j in range(CONV_W):
            wj = wb_ref[j * 8:(j + 1) * 8, :]
            p, s = divmod(lead + j, 8)
            for a in range(CONV_ROWS // 8):
                rows = pl.ds(base + 8 * (a + p), 8)
                tap = buf[rows, :] if s == 0 else shifted[s - 1, rows, :]
                accs[a] = accs[a] + wj * tap
        for a in range(CONV_ROWS // 8):
            y_ref[pl.ds(base + a * 8, 8), :] = accs[a]
        return carry

    lax.fori_loop(0, tc // CONV_ROWS, rows_body, 0)
    buf[0:CONV_HIST, :] = buf[tc:tc + CONV_HIST, :]


def _conv_prompt(z3, conv_w, conv_b, tc):
    bsz, t, _ = z3.shape
    wb = jnp.repeat(conv_w, 8, axis=0)
    return pl.pallas_call(
        functools.partial(_conv_prompt_kernel, tc=tc),
        grid=(bsz, t // tc),
        in_specs=[
            pl.BlockSpec((None, tc, D_MODEL), lambda b, i: (b, i, 0)),
            pl.BlockSpec((CONV_W * 8, D_MODEL), lambda b, i: (0, 0)),
            pl.BlockSpec((1, D_MODEL), lambda b, i: (0, 0)),
        ],
        out_specs=pl.BlockSpec((None, tc, D_MODEL), lambda b, i: (b, i, 0)),
        out_shape=jax.ShapeDtypeStruct((bsz, t, D_MODEL), F32),
        scratch_shapes=[pltpu.VMEM((CONV_HIST + tc, D_MODEL), F32),
                        pltpu.VMEM((7, CONV_HIST + tc - 8, D_MODEL), F32)],
        compiler_params=_cparams(("parallel", "arbitrary")),
        name="conv_prompt",
    )(z3, wb, conv_b.reshape(1, D_MODEL))


def _conv_sample_kernel(u_ref, w_ref, bias_ref, s_ref, *rest, bb):
    y_ref, so_ref = rest[-2:]
    hist = CONV_W - 1

    @pl.when(pl.program_id(0) == 0)
    def _():
        for b in range(bb):
            u = u_ref[b:b + 1, :]
            y = jnp.sum(s_ref[b] * w_ref[0:hist, :], axis=0, keepdims=True)
            y_ref[b:b + 1, :] = y + w_ref[hist:CONV_W, :] * u + bias_ref[...]
            so_ref[b, 0:hist - 1, :] = s_ref[b, 1:hist, :]
            so_ref[b, hist - 1:hist, :] = u

    @pl.when(pl.program_id(0) > 0)
    def _():
        so_ref[...] = jnp.zeros_like(so_ref)


def _conv_sample(z, states, layer, stacked, conv_w, conv_b, bb):
    n = z.shape[0]
    rows, blk, state_in_specs, state_operands, state_out_spec = _stacked_state_io(
        states, layer, stacked, (bb, CONV_W - 1, D_MODEL))
    n_in = 3 + len(state_operands)
    return pl.pallas_call(
        functools.partial(_conv_sample_kernel, bb=bb),
        grid=(rows, n // bb),
        in_specs=[
            pl.BlockSpec((bb, D_MODEL), lambda s, i: (blk(s, i), 0)),
            pl.BlockSpec((CONV_W, D_MODEL), lambda s, i: (0, 0)),
            pl.BlockSpec((1, D_MODEL), lambda s, i: (0, 0)),
        ] + state_in_specs,
        out_specs=[pl.BlockSpec((bb, D_MODEL), lambda s, i: (blk(s, i), 0)), state_out_spec],
        out_shape=[jax.ShapeDtypeStruct((n, D_MODEL), F32),
                   jax.ShapeDtypeStruct(states.shape, F32)],
        input_output_aliases={} if stacked is None else {n_in - 1: 1},
        compiler_params=_cparams(("arbitrary", "arbitrary")),
        name="conv_sample",
    )(z, conv_w, conv_b.reshape(1, D_MODEL), *state_operands)


def _merge_kernel(yc_ref, oa_ref, ga_ref, gb_ref, x_ref, g1_ref, sh2_ref, sc2_ref,
                  lng_ref, lnb_ref, n2g_ref, wco_ref, whg_ref, wo_ref, x1_ref, h2_ref):
    yc = yc_ref[...]
    xc = yc - jnp.mean(yc, axis=-1, keepdims=True)
    vln = xc * lax.rsqrt(jnp.mean(xc * xc, axis=-1, keepdims=True) + EPS)
    va = _silu(vln * lng_ref[...] + lnb_ref[...]).astype(BF16)
    y_b = jnp.dot(va, wco_ref[...], preferred_element_type=F32)
    y_a = jnp.dot(oa_ref[...], whg_ref[...], preferred_element_type=F32)
    mixed = (ga_ref[...] * y_a + gb_ref[...] * y_b).astype(BF16)
    x1 = x_ref[...] + g1_ref[...] * jnp.dot(mixed, wo_ref[...], preferred_element_type=F32)
    x1_ref[...] = x1
    y = x1 * lax.rsqrt(jnp.mean(x1 * x1, axis=-1, keepdims=True) + EPS) * n2g_ref[...]
    h2_ref[...] = (y * (1.0 + sc2_ref[...]) + sh2_ref[...]).astype(h2_ref.dtype)


def _merge(yc, oa, z, x, mod, ln_g, ln_b, n2_g, wco, whg, wo, layer, tm, rows_per_mod,
           h2_dtype=BF16):
    m = x.shape[0]
    mod_rows = mod.shape[1]

    def mod_spec(col):
        return pl.BlockSpec((None, mod_rows, D_MODEL),
                            lambda i: ((i * tm) // rows_per_mod, 0, col))

    def row_spec(col=0):
        return pl.BlockSpec((tm, D_MODEL), lambda i: (i, col))

    vec_spec = pl.BlockSpec((1, D_MODEL), lambda i: (0, 0))
    w_spec = pl.BlockSpec((None, D_MODEL, D_MODEL), lambda i: (layer, 0, 0))
    return pl.pallas_call(
        _merge_kernel,
        grid=(m // tm,),
        in_specs=[row_spec(), row_spec(), row_spec(4), row_spec(5), row_spec(),
                  mod_spec(2), mod_spec(3), mod_spec(4),
                  vec_spec, vec_spec, vec_spec, w_spec, w_spec, w_spec],
        out_specs=[row_spec(), row_spec()],
        out_shape=[jax.ShapeDtypeStruct((m, D_MODEL), F32),
                   jax.ShapeDtypeStruct((m, D_MODEL), h2_dtype)],
        compiler_params=_cparams(("parallel",)),
        name="merge",
    )(yc, oa, z, z, x, mod, mod, mod,
      ln_g.reshape(1, D_MODEL), ln_b.reshape(1, D_MODEL), n2_g.reshape(1, D_MODEL),
      wco, whg, wo)


def _swiglu_rows(x, wg_ref, wu_ref, wd_ref, tf, between=None):
    acc = None
    for j in range(D_FF // tf):
        if between is not None:
            between(j)
        cols = slice(j * tf, (j + 1) * tf)
        a = _silu(jnp.dot(x, wg_ref[:, cols], preferred_element_type=F32))
        a = a * jnp.dot(x, wu_ref[:, cols], preferred_element_type=F32)
        d = jnp.dot(a.astype(BF16), wd_ref[cols, :], preferred_element_type=F32)
        acc = d if acc is None else acc + d
    return acc


def _ffn_kernel(h_ref, x1_ref, g2_ref, wg_ref, wu_ref, wd_ref, o_ref, *, tf):
    o_ref[...] = x1_ref[...] + g2_ref[...] * _swiglu_rows(h_ref[...], wg_ref, wu_ref, wd_ref, tf)


def _ffn(h2, x1, mod, wg, wu, wd, tm, tf, rows_per_mod):
    m = x1.shape[0]
    mod_rows = mod.shape[1]
    w_up_spec = pl.BlockSpec((None, D_MODEL, D_FF), lambda i: (0, 0, 0), pipeline_mode=pl.Buffered(1))
    return pl.pallas_call(
        functools.partial(_ffn_kernel, tf=tf),
        grid=(m // tm,),
        in_specs=[
            pl.BlockSpec((tm, D_MODEL), lambda i: (i, 0)),
            pl.BlockSpec((tm, D_MODEL), lambda i: (i, 0)),
            pl.BlockSpec((None, mod_rows, D_MODEL), lambda i: ((i * tm) // rows_per_mod, 0, 5)),
            w_up_spec, w_up_spec,
            pl.BlockSpec((None, D_FF, D_MODEL), lambda i: (0, 0, 0), pipeline_mode=pl.Buffered(1)),
        ],
        out_specs=pl.BlockSpec((tm, D_MODEL), lambda i: (i, 0)),
        out_shape=jax.ShapeDtypeStruct((m, D_MODEL), F32),
        compiler_params=_cparams(("parallel",)),
        name="ffn",
    )(h2, x1, mod, wg, wu, wd)


def _top2(logits, lane):
    neg = jnp.float32(-jnp.inf)
    logits = jnp.where(lane < N_EXPERTS, logits, neg)
    v1 = jnp.max(logits, axis=-1, keepdims=True)
    i1 = jnp.min(jnp.where(logits == v1, lane, LANES), axis=-1, keepdims=True)
    rest = jnp.where(lane == i1, neg, logits)
    v2 = jnp.max(rest, axis=-1, keepdims=True)
    i2 = jnp.min(jnp.where(rest == v2, lane, LANES), axis=-1, keepdims=True)
    e2 = jnp.exp(v2 - v1)
    w1 = 1.0 / (1.0 + e2)
    return i1, i2, w1, e2 * w1


def _route_kernel(h_ref, rwt_ref, r_ref):
    logits = lax.dot_general(rwt_ref[...], h_ref[...].astype(BF16), (((1,), (1,)), ((), ())),
                             preferred_element_type=F32)
    expert = lax.broadcasted_iota(jnp.int32, logits.shape, 0)
    neg = jnp.float32(-jnp.inf)
    v1 = jnp.max(logits, axis=0, keepdims=True)
    i1 = jnp.min(jnp.where(logits == v1, expert, N_EXPERTS), axis=0, keepdims=True)
    rest = jnp.where(expert == i1, neg, logits)
    v2 = jnp.max(rest, axis=0, keepdims=True)
    i2 = jnp.min(jnp.where(rest == v2, expert, N_EXPERTS), axis=0, keepdims=True)
    e2 = jnp.exp(v2 - v1)
    w1 = 1.0 / (1.0 + e2)
    packed = (jnp.where(expert == 0, i1.astype(F32), 0.0) + jnp.where(expert == 1, i2.astype(F32), 0.0)
              + jnp.where(expert == 2, w1, 0.0) + jnp.where(expert == 3, e2 * w1, 0.0))
    r_ref[...] = jnp.zeros_like(r_ref)
    r_ref[:, 0:N_EXPERTS] = packed.T


def _route(h2, router_t, tm):
    m = h2.shape[0]
    return pl.pallas_call(
        _route_kernel,
        grid=(m // tm,),
        in_specs=[pl.BlockSpec((tm, D_MODEL), lambda i: (i, 0)),
                  pl.BlockSpec((N_EXPERTS, D_MODEL), lambda i: (0, 0))],
        out_specs=pl.BlockSpec((tm, LANES), lambda i: (i, 0)),
        out_shape=jax.ShapeDtypeStruct((m, LANES), F32),
        compiler_params=_cparams(("parallel",)),
        name="route",
    )(h2, router_t)


def _start_row_gather(idx_ref, n_rows, src_hbm, dst, sem):
    for r in range(n_rows):
        pltpu.make_async_copy(src_hbm.at[pl.ds(idx_ref[r], 1), :], dst.at[pl.ds(r, 1), :],
                              sem).start(priority=r % 2)


def _moe_dispatch_kernel(lo_ref, hi_ref, pos_ref, h_ref, x_hbm, stage, zero_row, sem, pad_sem):
    i = pl.program_id(0)
    last = pl.num_programs(0) - 1
    slot = lax.rem(i, 2)
    tb = stage.shape[1]

    def wait_block(s):
        for _ in range(2):
            pltpu.make_async_copy(stage.at[s], stage.at[s], sem.at[s]).wait()

    @pl.when(i >= 2)
    def _():
        wait_block(slot)

    stage[slot] = h_ref[...]

    for r in range(tb):
        row = stage.at[slot, pl.ds(r, 1), :]
        pltpu.make_async_copy(row, x_hbm.at[pl.ds(pos_ref[0, r], 1), :],
                              sem.at[slot]).start(priority=0)
        pltpu.make_async_copy(row, x_hbm.at[pl.ds(pos_ref[0, tb + r], 1), :],
                              sem.at[slot]).start(priority=1)

    @pl.when(i == last)
    def _():
        zero_row[...] = jnp.zeros_like(zero_row)
        pad_copy = lambda r: pltpu.make_async_copy(zero_row.at[pl.ds(0, 1), :],
                                                   x_hbm.at[pl.ds(r, 1), :], pad_sem.at[0])

        def start(r, carry):
            pad_copy(r).start()
            return carry

        def wait(r, carry):
            pad_copy(r).wait()
            return carry

        for e in range(lo_ref.shape[0]):
            lax.fori_loop(lo_ref[e], hi_ref[e], start, 0)
        for e in range(lo_ref.shape[0]):
            lax.fori_loop(lo_ref[e], hi_ref[e], wait, 0)
        wait_block(1 - slot)
        wait_block(slot)


def _moe_dispatch(h2, pos_blocks, pad_lo, pad_hi, n_rows):
    n_blocks = pos_blocks.shape[0]
    tb = pos_blocks.shape[2] // 2
    assert n_blocks >= 2
    grid_spec = pltpu.PrefetchScalarGridSpec(
        num_scalar_prefetch=2,
        grid=(n_blocks,),
        in_specs=[
            pl.BlockSpec((None, 1, 2 * tb), lambda i, lo, hi: (i, 0, 0), memory_space=pltpu.SMEM),
            pl.BlockSpec((tb, D_MODEL), lambda i, lo, hi: (i, 0)),
        ],
        out_specs=pl.BlockSpec(memory_space=pl.ANY),
        scratch_shapes=[pltpu.VMEM((2, tb, D_MODEL), F32), pltpu.VMEM((8, D_MODEL), F32),
                        pltpu.SemaphoreType.DMA((2,)), pltpu.SemaphoreType.DMA((1,))],
    )
    return pl.pallas_call(
        _moe_dispatch_kernel,
        grid_spec=grid_spec,
        out_shape=jax.ShapeDtypeStruct((n_rows, D_MODEL), F32),
        compiler_params=_cparams(("arbitrary",)),
        name="moe_dispatch",
    )(pad_lo, pad_hi, pos_blocks, h2)


def _moe_tiles_kernel(te_ref, na_ref, x_ref, wg_ref, wu_ref, wd_ref, y_ref, *, tf):
    @pl.when(pl.program_id(0) < na_ref[0])
    def _():
        y_ref[...] = _swiglu_rows(x_ref[...].astype(BF16), wg_ref, wu_ref, wd_ref, tf)

    @pl.when(pl.program_id(0) >= na_ref[0])
    def _():
        y_ref[...] = jnp.zeros_like(y_ref)


def _moe_tiles(x_sorted, tile_expert, n_active, wg, wu, wd, tm, tf):
    n_tiles = x_sorted.shape[0] // tm
    w_up_spec = pl.BlockSpec((None, None, D_MODEL, D_FF), lambda i, te, na: (0, te[i], 0, 0))
    grid_spec = pltpu.PrefetchScalarGridSpec(
        num_scalar_prefetch=2,
        grid=(n_tiles,),
        in_specs=[
            pl.BlockSpec((tm, D_MODEL), lambda i, te, na: (jnp.minimum(i, na[0] - 1), 0)),
            w_up_spec, w_up_spec,
            pl.BlockSpec((None, None, D_FF, D_MODEL), lambda i, te, na: (0, te[i], 0, 0)),
        ],
        out_specs=pl.BlockSpec((tm, D_MODEL), lambda i, te, na: (i, 0)),
    )
    return pl.pallas_call(
        functools.partial(_moe_tiles_kernel, tf=tf),
        grid_spec=grid_spec,
        out_shape=jax.ShapeDtypeStruct((n_tiles * tm, D_MODEL), F32),
        compiler_params=pltpu.CompilerParams(dimension_semantics=("arbitrary",),
                                             vmem_limit_bytes=MOE_VMEM_LIMIT),
        name="moe_tiles",
    )(tile_expert, n_active, x_sorted, wg, wu, wd)


def _moe_combine_kernel(pos_ref, pos_next_ref, y_hbm, r_ref, x1_ref, g2_ref, fg_ref, o_ref,
                        buf, sem):
    i = pl.program_id(0)
    slot = lax.rem(i, 2)
    n_rows = buf.shape[1]

    @pl.when(i == 0)
    def _():
        _start_row_gather(pos_ref.at[0], n_rows, y_hbm, buf.at[0], sem.at[0])

    @pl.when(i + 1 < pl.num_programs(0))
    def _():
        _start_row_gather(pos_next_ref.at[0], n_rows, y_hbm, buf.at[1 - slot], sem.at[1 - slot])

    pltpu.make_async_copy(buf.at[slot], buf.at[slot], sem.at[slot]).wait()
    tc = n_rows // 2
    lane = lax.broadcasted_iota(jnp.int32, r_ref.shape, 1)
    r = r_ref[...]
    w1 = jnp.sum(jnp.where(lane == 2, r, 0.0), axis=-1, keepdims=True)
    w2 = jnp.sum(jnp.where(lane == 3, r, 0.0), axis=-1, keepdims=True)
    f = w1 * buf[slot, 0:tc, :] + w2 * buf[slot, tc:n_rows, :]
    x2 = x1_ref[...] + g2_ref[...] * f
    o_ref[...] = x2 * lax.rsqrt(jnp.mean(x2 * x2, axis=-1, keepdims=True) + EPS) * fg_ref[...]


def _moe_combine(y_sorted, pos, route, x1, mod, final_g, tc, rows_per_mod):
    m = x1.shape[0]
    n_blocks = m // tc
    mod_rows = mod.shape[1]
    pos_spec = lambda f: pl.BlockSpec((None, 1, 2 * tc), f, memory_space=pltpu.SMEM)
    return pl.pallas_call(
        _moe_combine_kernel,
        grid=(n_blocks,),
        in_specs=[
            pos_spec(lambda i: (i, 0, 0)),
            pos_spec(lambda i: (jnp.minimum(i + 1, n_blocks - 1), 0, 0)),
            pl.BlockSpec(memory_space=pl.ANY),
            pl.BlockSpec((tc, LANES), lambda i: (i, 0)),
            pl.BlockSpec((tc, D_MODEL), lambda i: (i, 0)),
            pl.BlockSpec((None, mod_rows, D_MODEL), lambda i: ((i * tc) // rows_per_mod, 0, 5)),
            pl.BlockSpec((1, D_MODEL), lambda i: (0, 0)),
        ],
        out_specs=pl.BlockSpec((tc, D_MODEL), lambda i: (i, 0)),
        out_shape=jax.ShapeDtypeStruct((m, D_MODEL), F32),
        scratch_shapes=[pltpu.VMEM((2, 2 * tc, D_MODEL), F32), pltpu.SemaphoreType.DMA((2,))],
        compiler_params=_cparams(("arbitrary",)),
        name="moe_combine",
    )(pos, pos, y_sorted, route, x1, mod, final_g.reshape(1, D_MODEL))


def _moe_routed(h2, x1, mod, router_t, wg, wu, wd, final_g, tm, tf, tc, rows_per_mod):
    n = h2.shape[0]
    route = _route(h2, router_t, 512)
    expert = route[:, :2].astype(jnp.int32).reshape(2 * n)
    onehot = (expert[:, None] == jnp.arange(N_EXPERTS, dtype=jnp.int32)[None, :]).astype(jnp.int32)
    running = jnp.cumsum(onehot, axis=0)
    rank = jnp.sum((running - onehot) * onehot, axis=1)
    count = running[-1]
    padded = ((count + tm - 1) // tm) * tm
    group_end = jnp.cumsum(padded)
    group_start = group_end - padded
    pos = group_start[expert] + rank
    n_tiles = (2 * n) // tm + N_EXPERTS
    n_rows = n_tiles * tm
    tile_start = jnp.arange(n_tiles, dtype=jnp.int32) * tm
    tile_expert = jnp.minimum(jnp.sum(tile_start[:, None] >= group_end[None, :], axis=1),
                              N_EXPERTS - 1).astype(jnp.int32)
    n_active = (group_end[-1:] // tm).astype(jnp.int32)
    pad_lo = jnp.concatenate([group_start + count, group_end[-1:]]).astype(jnp.int32)
    pad_hi = jnp.concatenate([group_end, jnp.full((1,), n_rows)]).astype(jnp.int32)
    pos_blocks = pos.reshape(n // tc, tc, 2).transpose(0, 2, 1).reshape(n // tc, 1, 2 * tc)

    x_sorted = _moe_dispatch(h2, pos_blocks, pad_lo, pad_hi, n_rows)
    y_sorted = _moe_tiles(x_sorted, tile_expert, n_active, wg, wu, wd, tm, tf)
    return _moe_combine(y_sorted, pos_blocks, route, x1, mod, final_g, tc, rows_per_mod)


def _moe_kernel(h_ref, x1_ref, g2_ref, rw_ref, wg_ref, wu_ref, wd_ref, fg_ref, o_ref,
                acc, comb):
    e = pl.program_id(1)
    j = pl.program_id(2)
    lane = lax.broadcasted_iota(jnp.int32, comb.shape, 1)

    @pl.when((e == 0) & (j == 0))
    def _():
        acc[...] = jnp.zeros_like(acc)
        i1, i2, w1, w2 = _top2(jnp.dot(h_ref[...], rw_ref[...], preferred_element_type=F32), lane)
        comb[...] = jnp.where(lane == i1, w1, 0.0) + jnp.where(lane == i2, w2, 0.0)

    ce = jnp.sum(jnp.where(lane == e, comb[...], 0.0), axis=-1, keepdims=True)
    h = h_ref[...]
    a = _silu(jnp.dot(h, wg_ref[...], preferred_element_type=F32))
    a = a * jnp.dot(h, wu_ref[...], preferred_element_type=F32) * ce
    acc[...] += jnp.dot(a.astype(BF16), wd_ref[...], preferred_element_type=F32)

    @pl.when((e == pl.num_programs(1) - 1) & (j == pl.num_programs(2) - 1))
    def _():
        x2 = x1_ref[...] + g2_ref[...] * acc[...]
        o_ref[...] = x2 * lax.rsqrt(jnp.mean(x2 * x2, axis=-1, keepdims=True) + EPS) * fg_ref[...]


def _moe(h2, x1, mod, router_pad, wg, wu, wd, final_g, tm, tf, rows_per_mod):
    m = x1.shape[0]
    mod_rows = mod.shape[1]
    return pl.pallas_call(
        _moe_kernel,
        grid=(m // tm, N_EXPERTS, D_FF // tf),
        in_specs=[
            pl.BlockSpec((tm, D_MODEL), lambda i, e, j: (i, 0)),
            pl.BlockSpec((tm, D_MODEL), lambda i, e, j: (i, 0)),
            pl.BlockSpec((None, mod_rows, D_MODEL),
                         lambda i, e, j: ((i * tm) // rows_per_mod, 0, 5)),
            pl.BlockSpec((D_MODEL, LANES), lambda i, e, j: (0, 0)),
            pl.BlockSpec((None, None, D_MODEL, tf), lambda i, e, j: (0, e, 0, j)),
            pl.BlockSpec((None, None, D_MODEL, tf), lambda i, e, j: (0, e, 0, j)),
            pl.BlockSpec((None, None, tf, D_MODEL), lambda i, e, j: (0, e, j, 0)),
            pl.BlockSpec((1, D_MODEL), lambda i, e, j: (0, 0)),
        ],
        out_specs=pl.BlockSpec((tm, D_MODEL), lambda i, e, j: (i, 0)),
        out_shape=jax.ShapeDtypeStruct((m, D_MODEL), F32),
        scratch_shapes=[pltpu.VMEM((tm, D_MODEL), F32), pltpu.VMEM((tm, LANES), F32)],
        compiler_params=_cparams(("parallel", "arbitrary", "arbitrary")),
        name="moe",
    )(h2, x1, mod, router_pad, wg, wu, wd, final_g.reshape(1, D_MODEL))


def kernel(x_prompt, x_sample, state_hgrn, state_conv, c_prompt, c_sample, ada_w, ada_b, norm1_g, norm2_g, w_in, lb_logits, hg_norm_g, w_hg_out, conv_w, conv_b, conv_ln_g, conv_ln_b, w_conv_out, w_o, ffn_w_gate, ffn_w_up, ffn_w_down, router_w, moe_w_gate, moe_w_up, moe_w_down, final_norm_g):
    bsz, seq, _ = x_prompt.shape
    n_s = x_sample.shape[0]
    hist = CONV_W - 1

    w_in_b = w_in.astype(BF16)
    whg_b = w_hg_out.astype(BF16)
    wco_b = w_conv_out.astype(BF16)
    wo_b = w_o.astype(BF16)
    ffn_g, ffn_u, ffn_d = (w.astype(BF16) for w in (ffn_w_gate, ffn_w_up, ffn_w_down))
    moe_g, moe_u, moe_d = (w.astype(BF16) for w in (moe_w_gate, moe_w_up, moe_w_down))
    router_pad = jnp.pad(router_w[0], ((0, 0), (0, LANES - N_EXPERTS))).astype(BF16)
    router_t = router_w[0].T.astype(BF16)
    acat, lvl = _hgrn_constants()

    mod_all = _ada(jnp.concatenate([c_prompt, c_sample], axis=0), ada_w, ada_b)

    xp = x_prompt.reshape(bsz * seq, D_MODEL)
    xs = x_sample.reshape(n_s, D_MODEL)
    tm_p = 256
    hgrn_p, conv_p = [], []
    hgrn_s = conv_s = None
    for l in range(DEPTH):
        mod_p = mod_all[l, :bsz].reshape(bsz, 1, 6 * D_MODEL)
        mod_s = mod_all[l, bsz:].reshape(1, n_s, 6 * D_MODEL)

        zp, up, lfp = _inproj(xp, mod_p, norm1_g[l], lb_logits, w_in_b, l, tm_p, seq, BF16)
        zs, us, lfs = _inproj(xs, mod_s, norm1_g[l], lb_logits, w_in_b, l, n_s, n_s, F32)

        zp3 = zp.reshape(bsz, seq, N_GROUPS_OUT * D_MODEL)
        up3 = up.reshape(bsz, seq, D_MODEL)
        oa_p, s_p = _hgrn_prompt(zp3, lfp.reshape(bsz, seq, D_MODEL), hg_norm_g[l], acat, lvl, 512, 2)
        oa_s, hgrn_s = _hgrn_sample(zs, lfs, hg_norm_g[l], state_hgrn, l, hgrn_s, 8)

        yc_p = _conv_prompt(up3, conv_w[l], conv_b[l], 256)
        yc_s, conv_s = _conv_sample(us, state_conv, l, conv_s, conv_w[l], conv_b[l], 8)
        c_p = up3[:, seq - hist:, :]

        merge_w = (conv_ln_g[l], conv_ln_b[l], norm2_g[l], wco_b, whg_b, wo_b, l)
        dense = l % 2 == 0
        x1p, h2p = _merge(yc_p.reshape(bsz * seq, D_MODEL), oa_p.reshape(bsz * seq, D_MODEL),
                          zp, xp, mod_p, *merge_w, 512, seq, BF16 if dense else F32)
        x1s, h2s = _merge(yc_s, oa_s, zs, xs, mod_s, *merge_w, n_s, n_s)

        if dense:
            xp = _ffn(h2p, x1p, mod_p, ffn_g, ffn_u, ffn_d, 512, 256, seq)
            xs = _ffn(h2s, x1s, mod_s, ffn_g, ffn_u, ffn_d, n_s, 256, n_s)
        else:
            xp = _moe_routed(h2p, x1p, mod_p, router_t, moe_g, moe_u, moe_d, final_norm_g,
                             512, 256, 256, seq)
            xs = _moe(h2s, x1s, mod_s, router_pad, moe_g, moe_u, moe_d, final_norm_g, n_s, 1408, n_s)

        hgrn_p.append(s_p)
        conv_p.append(c_p)

    return (xp.reshape(bsz, seq, D_MODEL), xs.reshape(n_s, 1, D_MODEL),
            jnp.stack(hgrn_p), jnp.stack(conv_p), hgrn_s, conv_s)
```
